```python
import jax, jax.numpy as jnp
from jax import lax
import numpy as np

D_MODEL = 2048
BATCH = 4
SEQ = 4096
DEPTH = 2

POOL_WINDOWS = (2, 4, 8, 16)
POOL_GROUPS = 4
POOL_GROUP_DIM = D_MODEL // 16
POOL_WIDTH = POOL_GROUPS * POOL_GROUP_DIM
GDN_HEAD_DIM = 128
GDN_HEADS = (3 * D_MODEL // 8) // GDN_HEAD_DIM
GDN_WIDTH = GDN_HEADS * GDN_HEAD_DIM
GDN_CONV = 4
GDN_CHUNK = 64
LRU_BLOCK_DIM = 128
LRU_WIDTH = D_MODEL - POOL_WIDTH - GDN_WIDTH
LRU_BLOCKS = LRU_WIDTH // LRU_BLOCK_DIM
LRU_CONV = 4
LRU_C = 8.0
IN_SIZES = (POOL_WIDTH, GDN_WIDTH, GDN_WIDTH, GDN_WIDTH, GDN_WIDTH, GDN_HEADS, GDN_HEADS, LRU_WIDTH, LRU_WIDTH)
IN_COLS = POOL_WIDTH + 4 * GDN_WIDTH + 2 * GDN_HEADS + 2 * LRU_WIDTH
D_FF = 3 * D_MODEL
FFN_CONV = 3
EPS = 1e-6

kernel_name = 'hymba_pool_gdn_rglru_convffn'


def rmsnorm(x, w):
    xf = x.astype(jnp.float32)
    y = xf * lax.rsqrt(jnp.mean(xf * xf, axis=-1, keepdims=True) + EPS)
    return (y * w.astype(jnp.float32)).astype(x.dtype)


def l2norm(t):
    return t * lax.rsqrt(jnp.sum(t * t, axis=-1, keepdims=True) + EPS)


def causal_dwconv(x, w):
    k = w.shape[0]
    return lax.conv_general_dilated(
        x, w[:, None, :].astype(x.dtype), window_strides=(1,), padding=[(k - 1, 0)],
        dimension_numbers=('NWC', 'WIO', 'NWC'), feature_group_count=x.shape[-1])


def split_points():
    return np.cumsum(np.array(IN_SIZES))[:-1].tolist()


def pool_mixer(u, w, b, scale):
    bsz, s, _ = u.shape
    uf = u.astype(jnp.float32).reshape(bsz, s, POOL_GROUPS, POOL_GROUP_DIM)
    cs = jnp.cumsum(uf, axis=1)
    pos = jnp.arange(s)
    outs = []
    for g, win in enumerate(POOL_WINDOWS):
        c = cs[:, :, g]
        lag = jnp.pad(c, ((0, 0), (win, 0), (0, 0)))[:, :s]
        cnt = jnp.minimum(pos + 1, win).astype(jnp.float32)[None, :, None]
        outs.append((c - lag) / cnt - uf[:, :, g])
    d = jnp.stack(outs, axis=2).astype(u.dtype)
    y = jnp.einsum('bsgc,gcd->bsgd', d, w) + b
    return y.reshape(bsz, s, POOL_WIDTH) * scale


def gated_deltanet(q, k, v, z, a, bt, conv_w, a_log, dt_bias, norm_w):
    bsz, s, _ = q.shape
    H, Dh, C = GDN_HEADS, GDN_HEAD_DIM, GDN_CHUNK
    N = s // C
    out_dtype = z.dtype
    qkv = jax.nn.silu(causal_dwconv(jnp.concatenate([q, k, v], axis=-1), conv_w)).astype(jnp.float32)
    q, k, v = jnp.split(qkv, 3, axis=-1)

    def heads(t):
        return t.reshape(bsz, N, C, H, Dh).transpose(0, 3, 1, 2, 4)

    def per_head(t):
        return t.reshape(bsz, N, C, H).transpose(0, 3, 1, 2)

    q = l2norm(heads(q)) * (Dh ** -0.5)
    k = l2norm(heads(k))
    v = heads(v)
    beta = jax.nn.sigmoid(per_head(bt.astype(jnp.float32)))
    g = -jnp.exp(a_log.astype(jnp.float32)) * jax.nn.softplus(a.astype(jnp.float32) + dt_bias.astype(jnp.float32))
    g = jnp.cumsum(per_head(g), axis=-1)

    causal = jnp.tril(jnp.ones((C, C), dtype=bool))
    strict = jnp.tril(jnp.ones((C, C), dtype=bool), -1)
    decay = jnp.exp(jnp.where(causal, g[..., :, None] - g[..., None, :], -jnp.inf))
    kk = jnp.einsum('bhncd,bhnsd->bhncs', k, k)
    m = jnp.where(strict, beta[..., None] * kk * decay, 0.0) + jnp.eye(C, dtype=jnp.float32)
    rhs = jnp.concatenate([k * (beta * jnp.exp(g))[..., None], v * beta[..., None]], axis=-1)
    wu = lax.linalg.triangular_solve(m, rhs, left_side=True, lower=True)
    w_c, u_c = wu[..., :Dh], wu[..., Dh:]
    attn = jnp.einsum('bhncd,bhnsd->bhncs', q, k) * decay
    g_last = g[..., -1]
    q_dec = q * jnp.exp(g)[..., None]
    k_dec = k * jnp.exp(g_last[..., None] - g)[..., None]

    def step(state, xs):
        qd, kd, wc, uc, at, gl = xs
        v_new = uc - jnp.einsum('bhcd,bhde->bhce', wc, state)
        o = jnp.einsum('bhcd,bhde->bhce', qd, state) + jnp.einsum('bhcs,bhse->bhce', at, v_new)
        state = state * jnp.exp(gl)[..., None, None] + jnp.einsum('bhcd,bhce->bhde', kd, v_new)
        return state, o

    xs = tuple(jnp.moveaxis(t, 2, 0) for t in (q_dec, k_dec, w_c, u_c, attn, g_last))
    s0 = jnp.zeros((bsz, H, Dh, Dh), jnp.float32)
    _, o = lax.scan(step, s0, xs)
    o = o.transpose(1, 0, 3, 2, 4).reshape(bsz, s, H, Dh)
    zf = z.astype(jnp.float32).reshape(bsz, s, H, Dh)
    o = o * lax.rsqrt(jnp.mean(o * o, axis=-1, keepdims=True) + EPS) * norm_w.astype(jnp.float32) * jax.nn.silu(zf)
    return o.reshape(bsz, s, GDN_WIDTH).astype(out_dtype)


def rglru_mixer(xb, gate, conv_w, conv_b, wa, ba, wx, bx, lam):
    bsz, s, _ = xb.shape
    xc = causal_dwconv(xb, conv_w) + conv_b
    xh = xc.reshape(bsz, s, LRU_BLOCKS, LRU_BLOCK_DIM)
    r = jax.nn.sigmoid((jnp.einsum('bshc,hcd->bshd', xh, wa).reshape(bsz, s, LRU_WIDTH) + ba).astype(jnp.float32))
    i = jax.nn.sigmoid((jnp.einsum('bshc,hcd->bshd', xh, wx).reshape(bsz, s, LRU_WIDTH) + bx).astype(jnp.float32))
    log_a = -LRU_C * r * jax.nn.softplus(-lam.astype(jnp.float32))
    a = jnp.exp(log_a)
    mult = jnp.sqrt(-jnp.expm1(2.0 * log_a))
    pos = jnp.arange(s)[None, :, None]
    mult = jnp.where(pos == 0, 1.0, mult)
    b_in = mult * i * xc.astype(jnp.float32)

    def combine(lhs, rhs):
        return (lhs[0] * rhs[0], rhs[0] * lhs[1] + rhs[1])

    _, h = lax.associative_scan(combine, (a, b_in), axis=1)
    y = h * jax.nn.gelu(gate.astype(jnp.float32))
    return y.astype(xb.dtype)


def conv_ffn(h, w_up, conv_w, w_down):
    up = h @ w_up
    gate, val = jnp.split(up, 2, axis=-1)
    gate = causal_dwconv(gate, conv_w)
    return (jax.nn.gelu(gate) * val) @ w_down


def setup_inputs(seed: int = 0) -> dict:
    key = jax.random.key(seed)
    ks = jax.random.split(key, 24)
    f32 = jnp.float32
    L = DEPTH
    nrm = lambda k, shape, sc: jax.random.normal(k, shape, f32) * sc
    gain = lambda k, shape: 1.0 + 0.02 * jax.random.normal(k, shape, f32)
    dt = jnp.exp(jax.random.uniform(ks[8], (L, GDN_HEADS), f32, np.log(1e-3), np.log(1e-1)))
    a0 = jax.random.uniform(ks[17], (L, LRU_WIDTH), f32, 0.9, 0.999) ** (1.0 / LRU_C)
    return {
        'x': nrm(ks[0], (BATCH, SEQ, D_MODEL), 1.0),
        'norm1_w': gain(ks[1], (L, D_MODEL)),
        'w_in': nrm(ks[2], (L, D_MODEL, IN_COLS), D_MODEL ** -0.5),
        'pool_w': nrm(ks[3], (L, POOL_GROUPS, POOL_GROUP_DIM, POOL_GROUP_DIM), POOL_GROUP_DIM ** -0.5),
        'pool_b': nrm(ks[4], (L, POOL_GROUPS, POOL_GROUP_DIM), 0.01),
        'pool_scale': 0.5 + 0.05 * jax.random.normal(ks[5], (L, POOL_WIDTH), f32),
        'gdn_conv_w': nrm(ks[6], (L, GDN_CONV, 3 * GDN_WIDTH), GDN_CONV ** -0.5),
        'gdn_a_log': jnp.log(jax.random.uniform(ks[7], (L, GDN_HEADS), f32, 1.0, 16.0)),
        'gdn_dt_bias': dt + jnp.log(-jnp.expm1(-dt)),
        'gdn_norm_w': gain(ks[9], (L, GDN_HEAD_DIM)),
        'lru_conv_w': nrm(ks[10], (L, LRU_CONV, LRU_WIDTH), LRU_CONV ** -0.5),
        'lru_conv_b': nrm(ks[11], (L, LRU_WIDTH), 0.01),
        'lru_wa': nrm(ks[12], (L, LRU_BLOCKS, LRU_BLOCK_DIM, LRU_BLOCK_DIM), LRU_BLOCK_DIM ** -0.5),
        'lru_ba': nrm(ks[13], (L, LRU_WIDTH), 0.01),
        'lru_wx': nrm(ks[14], (L, LRU_BLOCKS, LRU_BLOCK_DIM, LRU_BLOCK_DIM), LRU_BLOCK_DIM ** -0.5),
        'lru_bx': nrm(ks[15], (L, LRU_WIDTH), 0.01),
        'lru_lambda': jnp.log(a0 / (1.0 - a0)),
        'w_out': nrm(ks[16], (L, D_MODEL, D_MODEL), D_MODEL ** -0.5),
        'norm2_w': gain(ks[18], (L, D_MODEL)),
        'ffn_up': nrm(ks[19], (L, D_MODEL, 2 * D_FF), D_MODEL ** -0.5),
        'ffn_conv_w': nrm(ks[20], (L, FFN_CONV, D_FF), FFN_CONV ** -0.5),
        'ffn_down': nrm(ks[21], (L, D_FF, D_MODEL), D_FF ** -0.5),
        'final_norm_w': gain(ks[22], (D_MODEL,)),
    }


def reference(x, norm1_w, w_in, pool_w, pool_b, pool_scale, gdn_conv_w, gdn_a_log, gdn_dt_bias,
              gdn_norm_w, lru_conv_w, lru_conv_b, lru_wa, lru_ba, lru_wx, lru_bx, lru_lambda,
              w_out, norm2_w, ffn_up, ffn_conv_w, ffn_down, final_norm_w):
    cuts = split_points()
    for l in range(DEPTH):
        h = rmsnorm(x, norm1_w[l])
        proj = h @ w_in[l]
        u_pool, q, k, v, z, a, bt, xr, gr = jnp.split(proj, cuts, axis=-1)
        y_pool = pool_mixer(u_pool, pool_w[l], pool_b[l], pool_scale[l])
        y_gdn = gated_deltanet(q, k, v, z, a, bt, gdn_conv_w[l], gdn_a_log[l], gdn_dt_bias[l], gdn_norm_w[l])
        y_lru = rglru_mixer(xr, gr, lru_conv_w[l], lru_conv_b[l], lru_wa[l], lru_ba[l],
                            lru_wx[l], lru_bx[l], lru_lambda[l])
        mixed = jnp.concatenate([y_pool.astype(x.dtype), y_gdn.astype(x.dtype), y_lru.astype(x.dtype)], axis=-1)
        x = x + mixed @ w_out[l]
        x = x + conv_ffn(rmsnorm(x, norm2_w[l]), ffn_up[l], ffn_conv_w[l], ffn_down[l])
    return rmsnorm(x, final_norm_w)
```

```python
import functools

import jax
import jax.numpy as jnp
from jax import lax
from jax.experimental import pallas as pl
from jax.experimental.pallas import tpu as pltpu

F32 = jnp.float32
BF16 = jnp.bfloat16

D_MODEL = 2048
DEPTH = 2
POOL_WINDOWS = (2, 4, 8, 16)
POOL_GROUP_DIM = 128
POOL_WIDTH = 512
GDN_HEADS = 6
HEAD_DIM = 128
GDN_WIDTH = 768
GDN_CONV = 4
GDN_CHUNK = 64
LRU_BLOCKS = 6
LRU_WIDTH = 768
LRU_CONV = 4
LRU_C = 8.0
D_FF = 3 * D_MODEL
FFN_CONV = 3
EPS = 1e-6

COL_Q, COL_K, COL_V, COL_Z = 0, 768, 1536, 2304
COL_XR, COL_GR = 3072, 3840
COL_POOL = 4608
COL_A, COL_BT = 5120, 5248
PROJ_COLS = 5376

HALO = 8
PAIR = 2 * GDN_CHUNK
NEG_BIG = -1e30

V7X_VMEM_LIMIT = 56 * 1024 * 1024


def _cparams(sem):
    return pltpu.CompilerParams(dimension_semantics=sem, vmem_limit_bytes=V7X_VMEM_LIMIT)


def _dot(a, b):
    return jnp.dot(a, b, preferred_element_type=F32)


def _dot_nt(a, b):
    return lax.dot_general(a, b, (((1,), (1,)), ((), ())), preferred_element_type=F32)


def _dot_tn(a, b):
    return lax.dot_general(a, b, (((0,), (0,)), ((), ())), preferred_element_type=F32)


def _split2(x):
    hi = x.astype(BF16)
    lo = (x - hi.astype(F32)).astype(BF16)
    return hi, lo


def _dot_hp(a, b):
    ah, al = _split2(a)
    bh, bl = _split2(b)
    return _dot(ah, bh) + _dot(ah, bl) + _dot(al, bh)


def _sigmoid(x):
    return 1.0 / (1.0 + jnp.exp(-x))


def _softplus(x):
    return jnp.maximum(x, 0.0) + jnp.log1p(jnp.exp(-jnp.abs(x)))


def _gelu(x):
    return jax.nn.gelu(x, approximate=True)


def _rmsnorm(x, w):
    return x * lax.rsqrt(jnp.mean(x * x, axis=-1, keepdims=True) + EPS) * w


def _conv_from_ext(ext_ref, col0, width, w, taps, rows):
    acc = None
    for j in range(taps):
        off = HALO - (taps - 1) + j
        term = ext_ref[pl.ds(off, rows), col0:col0 + width] * w[j:j + 1, :]
        acc = term if acc is None else acc + term
    return acc


def _inproj_kernel(x_ref, nw_ref, w_ref, o_ref, h_ref):
    @pl.when(pl.program_id(1) == 0)
    def _():
        h_ref[...] = _rmsnorm(x_ref[...], nw_ref[...]).astype(BF16)

    o_ref[...] = _dot(h_ref[...], w_ref[...])


def _inproj(x, nw, w, *, tm, tn):
    t, d = x.shape
    n = w.shape[1]
    return pl.pallas_call(
        _inproj_kernel,
        grid=(t // tm, n // tn),
        in_specs=[
            pl.BlockSpec((tm, d), lambda i, j: (i, 0)),
            pl.BlockSpec((1, d), lambda i, j: (0, 0)),
            pl.BlockSpec((d, tn), lambda i, j: (0, j)),
        ],
        out_specs=pl.BlockSpec((tm, tn), lambda i, j: (i, j)),
        out_shape=jax.ShapeDtypeStruct((t, n), F32),
        scratch_shapes=[pltpu.VMEM((tm, d), BF16)],
        compiler_params=_cparams(("arbitrary", "arbitrary")),
        name="inproj",
    )(x, nw, w)


def _pool_kernel(u_ref, w_ref, b_ref, sc_ref, o_ref, prev_ref):
    s = pl.program_id(1)
    tp = u_ref.shape[0]

    @pl.when(s == 0)
    def _():
        prev_ref[...] = jnp.zeros_like(prev_ref)

    r = lax.broadcasted_iota(jnp.int32, (tp, tp), 0)
    c = lax.broadcasted_iota(jnp.int32, (tp, tp), 1)
    lag = r - c
    rh = lax.broadcasted_iota(jnp.int32, (128, 128), 0)
    ch = lax.broadcasted_iota(jnp.int32, (128, 128), 1)
    lag_h = rh + 128 - ch
    pos = s * tp + lax.broadcasted_iota(jnp.int32, (tp, 1), 0)

    def split3(v):
        v1 = v.astype(BF16)
        r1 = v - v1.astype(F32)
        v2 = r1.astype(BF16)
        v3 = (r1 - v2.astype(F32)).astype(BF16)
        return v1, v2, v3

    for g, win in enumerate(POOL_WINDOWS):
        sl = slice(g * POOL_GROUP_DIM, (g + 1) * POOL_GROUP_DIM)
        u = u_ref[:, sl]
        up = prev_ref[:, sl]
        band = ((lag >= 0) & (lag < win)).astype(BF16)
        band_h = (lag_h < win).astype(BF16)
        u1, u2, u3 = split3(u)
        p1, p2, p3 = split3(up)
        wsum = _dot(band, u1) + _dot(band, u2) + _dot(band, u3)
        head = _dot(band_h, p1) + _dot(band_h, p2) + _dot(band_h, p3)
        wsum = jnp.concatenate([wsum[:128] + head, wsum[128:]], axis=0)
        cnt = jnp.minimum(pos + 1, win).astype(F32)
        d = wsum / cnt - u
        y = _dot(d.astype(BF16), w_ref[g]) + b_ref[:, sl]
        o_ref[:, sl] = (y * sc_ref[:, sl]).astype(o_ref.dtype)

    prev_ref[...] = u_ref[tp - 128:tp, :]


def _pool(proj, pw, pb, psc, *, batch, seq, tp):
    nblk = seq // tp
    return pl.pallas_call(
        _pool_kernel,
        grid=(batch, nblk),
        in_specs=[
            pl.BlockSpec((tp, POOL_WIDTH), lambda b, s: (b * nblk + s, COL_POOL // POOL_WIDTH)),
            pl.BlockSpec((4, 128, 128), lambda b, s: (0, 0, 0)),
            pl.BlockSpec((1, POOL_WIDTH), lambda b, s: (0, 0)),
            pl.BlockSpec((1, POOL_WIDTH), lambda b, s: (0, 0)),
        ],
        out_specs=pl.BlockSpec((tp, POOL_WIDTH), lambda b, s: (b * nblk + s, 0)),
        out_shape=jax.ShapeDtypeStruct((batch * seq, POOL_WIDTH), BF16),
        scratch_shapes=[pltpu.VMEM((128, POOL_WIDTH), F32)],
        compiler_params=_cparams(("arbitrary", "arbitrary")),
        name="pool",
    )(proj, pw, pb, psc)


def _lru_kernel(xr_ref, gr_ref, cw_ref, cb_ref, wa_ref, ba_ref, wx_ref, bx_ref, lam_ref,
                o_ref, ext_ref, hc_ref):
    s = pl.program_id(1)
    tl = xr_ref.shape[0]

    @pl.when(s == 0)
    def _():
        ext_ref[0:HALO, :] = jnp.zeros((HALO, LRU_WIDTH), F32)
        hc_ref[...] = jnp.zeros_like(hc_ref)

    @pl.when(s > 0)
    def _():
        ext_ref[0:HALO, :] = ext_ref[tl:tl + HALO, :]

    ext_ref[HALO:HALO + tl, :] = xr_ref[...]

    row = lax.broadcasted_iota(jnp.int32, (tl, 128), 0)
    first = (row == 0) & (s == 0)
    nsp = -LRU_C * _softplus(-lam_ref[...])
    cw = cw_ref[...]

    for blk in range(LRU_BLOCKS):
        sl = slice(blk * 128, (blk + 1) * 128)
        xc = _conv_from_ext(ext_ref, blk * 128, 128, cw[:, sl], LRU_CONV, tl) + cb_ref[:, sl]
        xb = xc.astype(BF16)
        r = _sigmoid(_dot(xb, wa_ref[blk]) + ba_ref[:, sl])
        i = _sigmoid(_dot(xb, wx_ref[blk]) + bx_ref[:, sl])
        log_a = r * nsp[:, sl]
        a = jnp.exp(log_a)
        mult = jnp.sqrt(-jnp.tanh(log_a) * (a * a + 1.0))
        mult = jnp.where(first, 1.0, mult)
        bv = mult * i * xc
        sh = 1
        while sh < tl:
            keep = row >= sh
            a_sh = jnp.where(keep, pltpu.roll(a, sh, 0), 1.0)
            b_sh = jnp.where(keep, pltpu.roll(bv, sh, 0), 0.0)
            bv = a * b_sh + bv
            a = a * a_sh
            sh *= 2
        h = a * hc_ref[:, sl] + bv
        hc_ref[:, sl] = h[tl - 1:tl, :]
        o_ref[:, sl] = (h * _gelu(gr_ref[:, sl])).astype(o_ref.dtype)


def _lru(proj, cw, cb, wa, ba, wx, bx, lam, *, batch, seq, tl):
    nblk = seq // tl
    vec = pl.BlockSpec((1, LRU_WIDTH), lambda b, s: (0, 0))
    mat = pl.BlockSpec((LRU_BLOCKS, 128, 128), lambda b, s: (0, 0, 0))
    return pl.pallas_call(
        _lru_kernel,
        grid=(batch, nblk),
        in_specs=[
            pl.BlockSpec((tl, LRU_WIDTH), lambda b, s: (b * nblk + s, COL_XR // LRU_WIDTH)),
            pl.BlockSpec((tl, LRU_WIDTH), lambda b, s: (b * nblk + s, COL_GR // LRU_WIDTH)),
            pl.BlockSpec((LRU_CONV, LRU_WIDTH), lambda b, s: (0, 0)),
            vec, mat, vec, mat, vec, vec,
        ],
        out_specs=pl.BlockSpec((tl, LRU_WIDTH), lambda b, s: (b * nblk + s, 0)),
        out_shape=jax.ShapeDtypeStruct((batch * seq, LRU_WIDTH), BF16),
        scratch_shapes=[pltpu.VMEM((tl + HALO, LRU_WIDTH), F32), pltpu.VMEM((1, LRU_WIDTH), F32)],
        compiler_params=_cparams(("arbitrary", "arbitrary")),
        name="lru",
    )(proj, proj, cw, cb, wa, ba, wx, bx, lam)


def _unit_lower_inverse(a_strict, masks):
    eye, m8, m16, m32 = masks
    n1 = jnp.where(m8, -a_strict, 0.0)
    p = eye + n1
    n2 = _dot_hp(n1, n1)
    p = p + _dot_hp(p, n2)
    n4 = _dot_hp(n2, n2)
    p = p + _dot_hp(p, n4)
    for inner, outer in ((m8, m16), (m16, m32), (m32, None)):
        sel = ~inner if outer is None else (outer & ~inner)
        e = jnp.where(sel, a_strict, 0.0)
        p = p - _dot_hp(_dot_hp(p, e), p)
    return p


def _gdn_kernel(q_ref, k_ref, v_ref, z_ref, a_ref, bt_ref, cw_ref, alog_ref, dtb_ref, nw_ref,
                o_ref, ext_ref, qs_ref, ks_ref, vs_ref, gc_ref, beta_ref, os_ref, st_ref):
    s = pl.program_id(1)
    ts = q_ref.shape[0]

    @pl.when(s == 0)
    def _():
        ext_ref[0:HALO, :] = jnp.zeros((HALO, 3 * GDN_WIDTH), F32)
        st_ref[...] = jnp.zeros_like(st_ref)

    @pl.when(s > 0)
    def _():
        ext_ref[0:HALO, :] = ext_ref[ts:ts + HALO, :]

    ext_ref[HALO:HALO + ts, 0:GDN_WIDTH] = q_ref[...]
    ext_ref[HALO:HALO + ts, GDN_WIDTH:2 * GDN_WIDTH] = k_ref[...]
    ext_ref[HALO:HALO + ts, 2 * GDN_WIDTH:3 * GDN_WIDTH] = v_ref[...]

    cw = cw_ref[...]
    for h in range(GDN_HEADS):
        for ti, dst in enumerate((qs_ref, ks_ref, vs_ref)):
            c0 = ti * GDN_WIDTH + h * HEAD_DIM
            t = _conv_from_ext(ext_ref, c0, HEAD_DIM, cw[:, c0:c0 + HEAD_DIM], GDN_CONV, ts)
            t = t * _sigmoid(t)
            if ti < 2:
                t = t * lax.rsqrt(jnp.sum(t * t, axis=-1, keepdims=True) + EPS)
            if ti == 0:
                t = t * (HEAD_DIM ** -0.5)
            dst[:, h * HEAD_DIM:(h + 1) * HEAD_DIM] = t

    graw = -jnp.exp(alog_ref[...]) * _softplus(a_ref[...] + dtb_ref[...])
    beta_ref[...] = _sigmoid(bt_ref[...])
    rr = lax.broadcasted_iota(jnp.int32, (ts, ts), 0)
    cc = lax.broadcasted_iota(jnp.int32, (ts, ts), 1)
    ltri = ((rr >= cc) & ((rr >> 6) == (cc >> 6))).astype(BF16)
    g1 = graw.astype(BF16)
    r1 = graw - g1.astype(F32)
    g2 = r1.astype(BF16)
    g3 = (r1 - g2.astype(F32)).astype(BF16)
    gc_ref[...] = _dot(ltri, g1) + _dot(ltri, g2) + _dot(ltri, g3)

    ri = lax.broadcasted_iota(jnp.int32, (PAIR, PAIR), 0)
    ci = lax.broadcasted_iota(jnp.int32, (PAIR, PAIR), 1)
    same_chunk = (ri >> 6) == (ci >> 6)
    causal = same_chunk & (ri >= ci)
    strict = same_chunk & (ri > ci)
    masks = (
        (ri == ci).astype(F32),
        (ri >> 3) == (ci >> 3),
        (ri >> 4) == (ci >> 4),
        (ri >> 5) == (ci >> 5),
    )
    top = lax.broadcasted_iota(jnp.int32, (PAIR, HEAD_DIM), 0) < GDN_CHUNK
    c_ = GDN_CHUNK

    def pair_body(p, carry):
        r0 = pl.multiple_of(p * PAIR, PAIR)
        rows = pl.ds(r0, PAIR)
        gcb = gc_ref[rows, :]
        g_t = gcb.T
        eg = jnp.exp(gcb)
        bet = beta_ref[rows, :]
        gl_a = gcb[c_ - 1:c_, :]
        gl_b = gcb[PAIR - 1:PAIR, :]
        ekd = jnp.exp(jnp.where(top, gl_a, gl_b) - gcb)
        egl_a = jnp.exp(gl_a)
        egl_b = jnp.exp(gl_b)
        for h in range(GDN_HEADS):
            sl = slice(h * HEAD_DIM, (h + 1) * HEAD_DIM)
            qh = qs_ref[rows, sl]
            kh = ks_ref[rows, sl]
            vh = vs_ref[rows, sl]
            gcol = gcb[:, h:h + 1]
            grow = g_t[h:h + 1, :]
            dec = jnp.exp(jnp.where(causal, gcol - grow, NEG_BIG))
            bcol = bet[:, h:h + 1]
            kb = kh.astype(BF16)
            kk = _dot_nt(kb, kb)
            a_strict = jnp.where(strict, bcol * kk * dec, 0.0)
            minv = _unit_lower_inverse(a_strict, masks)
            rhs = jnp.concatenate([kh * (bcol * eg[:, h:h + 1]), vh * bcol], axis=1)
            wu = _dot_hp(minv, rhs)
            wb = wu[:, :HEAD_DIM].astype(BF16)
            u = wu[:, HEAD_DIM:]
            attn = (_dot_nt(qh.astype(BF16), kb) * dec).astype(BF16)
            qd = (qh * eg[:, h:h + 1]).astype(BF16)
            kd = (kh * ekd[:, h:h + 1]).astype(BF16)
            s_a = st_ref[h]
            s_ab = s_a.astype(BF16)
            vn_a = u[:c_] - _dot(wb[:c_], s_ab)
            s_b = s_a * egl_a[:, h:h + 1] + _dot_tn(kd[:c_], vn_a.astype(BF16))
            s_bb = s_b.astype(BF16)
            vn_b = u[c_:] - _dot(wb[c_:], s_bb)
            st_ref[h] = s_b * egl_b[:, h:h + 1] + _dot_tn(kd[c_:], vn_b.astype(BF16))
            vn = jnp.concatenate([vn_a, vn_b], axis=0).astype(BF16)
            o_state = jnp.concatenate([_dot(qd[:c_], s_ab), _dot(qd[c_:], s_bb)], axis=0)
            os_ref[rows, sl] = o_state + _dot(attn, vn)
        return carry

    lax.fori_loop(0, ts // PAIR, pair_body, 0)

    nw = nw_ref[...]
    for h in range(GDN_HEADS):
        sl = slice(h * HEAD_DIM, (h + 1) * HEAD_DIM)
        o = os_ref[:, sl]
        zz = z_ref[:, sl]
        y = o * lax.rsqrt(jnp.mean(o * o, axis=-1, keepdims=True) + EPS) * nw * (zz * _sigmoid(zz))
        o_ref[:, sl] = y.astype(o_ref.dtype)


def _gdn(proj, cw, alog, dtb, nw, *, batch, seq, ts):
    nblk = seq // ts

    def col(c0, width):
        return pl.BlockSpec((ts, width), lambda b, s: (b * nblk + s, c0 // width))

    small = pl.BlockSpec((1, 128), lambda b, s: (0, 0))
    return pl.pallas_call(
        _gdn_kernel,
        grid=(batch, nblk),
        in_specs=[
            col(COL_Q, GDN_WIDTH), col(COL_K, GDN_WIDTH), col(COL_V, GDN_WIDTH), col(COL_Z, GDN_WIDTH),
            col(COL_A, 128), col(COL_BT, 128),
            pl.BlockSpec((GDN_CONV, 3 * GDN_WIDTH), lambda b, s: (0, 0)),
            small, small, small,
        ],
        out_specs=pl.BlockSpec((ts, GDN_WIDTH), lambda b, s: (b * nblk + s, 0)),
        out_shape=jax.ShapeDtypeStruct((batch * seq, GDN_WIDTH), BF16),
        scratch_shapes=[
            pltpu.VMEM((ts + HALO, 3 * GDN_WIDTH), F32),
            pltpu.VMEM((ts, GDN_WIDTH), F32),
            pltpu.VMEM((ts, GDN_WIDTH), F32),
            pltpu.VMEM((ts, GDN_WIDTH), F32),
            pltpu.VMEM((ts, 128), F32),
            pltpu.VMEM((ts, 128), F32),
            pltpu.VMEM((ts, GDN_WIDTH), F32),
            pltpu.VMEM((GDN_HEADS, HEAD_DIM, HEAD_DIM), F32),
        ],
        compiler_params=_cparams(("arbitrary", "arbitrary")),
        name="gdn",
    )(proj, proj, proj, proj, proj, proj, cw, alog, dtb, nw)


def _outproj_kernel(x_ref, yp_ref, yg_ref, yl_ref, wp_ref, wg_ref, wl_ref, o_ref):
    acc = _dot(yp_ref[...], wp_ref[...]) + _dot(yg_ref[...], wg_ref[...]) + _dot(yl_ref[...], wl_ref[...])
    o_ref[...] = x_ref[...] + acc


def _outproj(x, yp, yg, yl, wp, wg, wl, *, tm):
    t, d = x.shape

    def rows(width):
        return pl.BlockSpec((tm, width), lambda i: (i, 0))

    def whole(width):
        return pl.BlockSpec((width, d), lambda i: (0, 0))

    return pl.pallas_call(
        _outproj_kernel,
        grid=(t // tm,),
        in_specs=[rows(d), rows(POOL_WIDTH), rows(GDN_WIDTH), rows(LRU_WIDTH),
                  whole(POOL_WIDTH), whole(GDN_WIDTH), whole(LRU_WIDTH)],
        out_specs=rows(d),
        out_shape=jax.ShapeDtypeStruct((t, d), F32),
        compiler_params=_cparams(("arbitrary",)),
        name="outproj",
    )(x, yp, yg, yl, wp, wg, wl)


def _ffn_kernel(x_ref, nw_ref, wg_ref, wv_ref, wd_ref, cw_ref, fnw_ref, o_ref,
                h_ref, acc_ref, ext_ref, carry_ref, *, tiles_per_seq, final_norm):
    i = pl.program_id(0)
    f = pl.program_id(1)
    tm = x_ref.shape[0]

    @pl.when(f == 0)
    def _():
        h_ref[...] = _rmsnorm(x_ref[...], nw_ref[...]).astype(BF16)

    hb = h_ref[...]
    gate = _dot(hb, wg_ref[...])
    val = _dot(hb, wv_ref[...])

    seq_start = (i % tiles_per_seq) == 0

    @pl.when(seq_start)
    def _():
        ext_ref[0:HALO, :] = jnp.zeros((HALO, gate.shape[1]), F32)

    @pl.when(jnp.logical_not(seq_start))
    def _():
        ext_ref[0:HALO, :] = carry_ref[f]

    ext_ref[HALO:HALO + tm, :] = gate
    carry_ref[f] = gate[tm - HALO:tm, :]
    cw = cw_ref[...]
    conv = _conv_from_ext(ext_ref, 0, gate.shape[1], cw, FFN_CONV, tm)
    act = (_gelu(conv) * val).astype(BF16)
    contrib = _dot(act, wd_ref[...])

    @pl.when(f == 0)
    def _():
        acc_ref[...] = contrib

    @pl.when(f > 0)
    def _():
        acc_ref[...] += contrib

    @pl.when(f == pl.num_programs(1) - 1)
    def _():
        y = x_ref[...] + acc_ref[...]
        if final_norm:
            y = _rmsnorm(y, fnw_ref[...])
        o_ref[...] = y


def _ffn(x, nw, w_up, w_down, cw, fnw, *, seq, tm, tf, final_norm):
    t, d = x.shape
    nf = D_FF // tf
    kern = functools.partial(_ffn_kernel, tiles_per_seq=seq // tm, final_norm=final_norm)
    return pl.pallas_call(
        kern,
        grid=(t // tm, nf),
        in_specs=[
            pl.BlockSpec((tm, d), lambda i, f: (i, 0)),
            pl.BlockSpec((1, d), lambda i, f: (0, 0)),
            pl.BlockSpec((d, tf), lambda i, f: (0, f)),
            pl.BlockSpec((d, tf), lambda i, f: (0, f + nf)),
            pl.BlockSpec((tf, d), lambda i, f: (f, 0)),
            pl.BlockSpec((FFN_CONV, tf), lambda i, f: (0, f)),
            pl.BlockSpec((1, d), lambda i, f: (0, 0)),
        ],
        out_specs=pl.BlockSpec((tm, d), lambda i, f: (i, 0)),
        out_shape=jax.ShapeDtypeStruct((t, d), F32),
        scratch_shapes=[
            pltpu.VMEM((tm, d), BF16),
            pltpu.VMEM((tm, d), F32),
            pltpu.VMEM((tm + HALO, tf), F32),
            pltpu.VMEM((nf, HALO, tf), F32),
        ],
        compiler_params=_cparams(("arbitrary", "arbitrary")),
        name="ffn",
    )(x, nw, w_up, w_up, w_down, cw, fnw)


def _pad_cols(w, width):
    return jnp.pad(w, ((0, 0), (0, width - w.shape[1])))


def _reorder_w_in(w):
    o = 0
    parts = {}
    for name, size in (("pool", POOL_WIDTH), ("q", GDN_WIDTH), ("k", GDN_WIDTH), ("v", GDN_WIDTH),
                       ("z", GDN_WIDTH), ("a", GDN_HEADS), ("bt", GDN_HEADS),
                       ("xr", LRU_WIDTH), ("gr", LRU_WIDTH)):
        parts[name] = w[:, o:o + size]
        o += size
    cols = [parts["q"], parts["k"], parts["v"], parts["z"], parts["xr"], parts["gr"], parts["pool"],
            _pad_cols(parts["a"], 128), _pad_cols(parts["bt"], 128)]
    return jnp.concatenate(cols, axis=1).astype(BF16)


def _row(v, width=None):
    v = v.reshape(1, -1).astype(F32)
    return v if width is None else _pad_cols(v, width)


def kernel(x, norm1_w, w_in, pool_w, pool_b, pool_scale, gdn_conv_w, gdn_a_log, gdn_dt_bias, gdn_norm_w,
           lru_conv_w, lru_conv_b, lru_wa, lru_ba, lru_wx, lru_bx, lru_lambda, w_out, norm2_w, ffn_up,
           ffn_conv_w, ffn_down, final_norm_w):
    batch, seq, d = x.shape
    xf = x.reshape(batch * seq, d)
    for l in range(DEPTH):
        proj = _inproj(xf, _row(norm1_w[l]), _reorder_w_in(w_in[l]), tm=1024, tn=768)
        y_pool = _pool(proj, pool_w[l].astype(BF16), _row(pool_b[l]), _row(pool_scale[l]),
                       batch=batch, seq=seq, tp=256)
        y_gdn = _gdn(proj, gdn_conv_w[l], _row(gdn_a_log[l], 128), _row(gdn_dt_bias[l], 128),
                     _row(gdn_norm_w[l]), batch=batch, seq=seq, ts=256)
        y_lru = _lru(proj, lru_conv_w[l], _row(lru_conv_b[l]), lru_wa[l].astype(BF16), _row(lru_ba[l]),
                     lru_wx[l].astype(BF16), _row(lru_bx[l]), _row(lru_lambda[l]),
                     batch=batch, seq=seq, tl=256)
        wo = w_out[l].astype(BF16)
        x1 = _outproj(xf, y_pool, y_gdn, y_lru, wo[:POOL_WIDTH], wo[POOL_WIDTH:POOL_WIDTH + GDN_WIDTH],
                      wo[POOL_WIDTH + GDN_WIDTH:], tm=512)
        xf = _ffn(x1, _row(norm2_w[l]), ffn_up[l].astype(BF16), ffn_down[l].astype(BF16), ffn_conv_w[l],
                  _row(final_norm_w), seq=seq, tm=512, tf=512, final_norm=(l == DEPTH - 1))
    return xf.reshape(batch, seq, d)
```

```python
import functools

import jax
import jax.numpy as jnp
from jax import lax
from jax.experimental import pallas as pl
from jax.experimental.pallas import tpu as pltpu

F32 = jnp.float32
BF16 = jnp.bfloat16

D_MODEL = 2048
DEPTH = 2
POOL_WINDOWS = (2, 4, 8, 16)
POOL_GROUP_DIM = 128
POOL_WIDTH = 512
GDN_HEADS = 6
HEAD_DIM = 128
GDN_WIDTH = 768
GDN_CONV = 4
GDN_CHUNK = 64
LRU_BLOCKS = 6
LRU_WIDTH = 768
LRU_CONV = 4
LRU_C = 8.0
D_FF = 3 * D_MODEL
FFN_CONV = 3
EPS = 1e-6

COL_Q, COL_K, COL_V, COL_Z = 0, 768, 1536, 2304
COL_XR, COL_GR = 3072, 3840
COL_POOL = 4608
COL_AB = 5120
PROJ_COLS = 5376

HALO = 8
CHUNK_SHIFT = 7
CHUNK = 1 << CHUNK_SHIFT
NEG_BIG = -1e30

V7X_VMEM_LIMIT = 56 * 1024 * 1024


def _cparams(sem):
    return pltpu.CompilerParams(dimension_semantics=sem, vmem_limit_bytes=V7X_VMEM_LIMIT)


def _dot(a, b):
    return jnp.dot(a, b, preferred_element_type=F32)


def _dot_nt(a, b):
    return lax.dot_general(a, b, (((1,), (1,)), ((), ())), preferred_element_type=F32)


def _dot_tn(a, b):
    return lax.dot_general(a, b, (((0,), (0,)), ((), ())), preferred_element_type=F32)


def _sigmoid(x):
    return 0.5 + 0.5 * jnp.tanh(0.5 * x)


def _softplus(x):
    return jnp.maximum(x, 0.0) + jnp.log1p(jnp.exp(-jnp.abs(x)))


def _gelu(x):
    return jax.nn.gelu(x, approximate=True)


def _rmsnorm(x, w):
    return x * lax.rsqrt(jnp.mean(x * x, axis=-1, keepdims=True) + EPS) * w


def _conv_from_ext(ext_ref, col0, width, w, taps, rows):
    acc = None
    for j in range(taps):
        off = HALO - (taps - 1) + j
        term = ext_ref[pl.ds(off, rows), col0:col0 + width] * w[j:j + 1, :]
        acc = term if acc is None else acc + term
    return acc


def _inproj_kernel(x_ref, nw_ref, w_ref, o_ref, h_ref):
    @pl.when(pl.program_id(1) == 0)
    def _():
        h_ref[...] = _rmsnorm(x_ref[...], nw_ref[...]).astype(BF16)

    o_ref[...] = _dot(h_ref[...], w_ref[...])


def _inproj(x, nw, w, *, tm, tn):
    t, d = x.shape
    n = w.shape[1]
    return pl.pallas_call(
        _inproj_kernel,
        grid=(t // tm, n // tn),
        in_specs=[
            pl.BlockSpec((tm, d), lambda i, j: (i, 0)),
            pl.BlockSpec((1, d), lambda i, j: (0, 0)),
            pl.BlockSpec((d, tn), lambda i, j: (0, j)),
        ],
        out_specs=pl.BlockSpec((tm, tn), lambda i, j: (i, j)),
        out_shape=jax.ShapeDtypeStruct((t, n), F32),
        scratch_shapes=[pltpu.VMEM((tm, d), BF16)],
        compiler_params=_cparams(("arbitrary", "arbitrary")),
        name="inproj",
    )(x, nw, w)


def _pool_kernel(u_ref, w_ref, b_ref, sc_ref, o_ref, prev_ref):
    s = pl.program_id(1)
    tp = u_ref.shape[0]

    @pl.when(s == 0)
    def _():
        prev_ref[...] = jnp.zeros_like(prev_ref)

    r = lax.broadcasted_iota(jnp.int32, (tp, tp), 0)
    c = lax.broadcasted_iota(jnp.int32, (tp, tp), 1)
    lag = r - c
    rh = lax.broadcasted_iota(jnp.int32, (128, 128), 0)
    ch = lax.broadcasted_iota(jnp.int32, (128, 128), 1)
    lag_h = rh + 128 - ch
    pos = s * tp + lax.broadcasted_iota(jnp.int32, (tp, 1), 0)

    def split3(v):
        v1 = v.astype(BF16)
        r1 = v - v1.astype(F32)
        v2 = r1.astype(BF16)
        v3 = (r1 - v2.astype(F32)).astype(BF16)
        return v1, v2, v3

    for g, win in enumerate(POOL_WINDOWS):
        sl = slice(g * POOL_GROUP_DIM, (g + 1) * POOL_GROUP_DIM)
        u = u_ref[:, sl]
        up = prev_ref[:, sl]
        band = ((lag >= 0) & (lag < win)).astype(BF16)
        band_h = (lag_h < win).astype(BF16)
        u1, u2, u3 = split3(u)
        p1, p2, p3 = split3(up)
        wsum = _dot(band, u1) + _dot(band, u2) + _dot(band, u3)
        head = _dot(band_h, p1) + _dot(band_h, p2) + _dot(band_h, p3)
        wsum = jnp.concatenate([wsum[:128] + head, wsum[128:]], axis=0)
        cnt = jnp.minimum(pos + 1, win).astype(F32)
        d = wsum / cnt - u
        y = _dot(d.astype(BF16), w_ref[g]) + b_ref[:, sl]
        o_ref[:, sl] = (y * sc_ref[:, sl]).astype(o_ref.dtype)

    prev_ref[...] = u_ref[tp - 128:tp, :]


def _pool(proj, pw, pb, psc, *, batch, seq, tp):
    nblk = seq // tp
    return pl.pallas_call(
        _pool_kernel,
        grid=(batch, nblk),
        in_specs=[
            pl.BlockSpec((tp, POOL_WIDTH), lambda b, s: (b * nblk + s, COL_POOL // POOL_WIDTH)),
            pl.BlockSpec((4, 128, 128), lambda b, s: (0, 0, 0)),
            pl.BlockSpec((1, POOL_WIDTH), lambda b, s: (0, 0)),
            pl.BlockSpec((1, POOL_WIDTH), lambda b, s: (0, 0)),
        ],
        out_specs=pl.BlockSpec((tp, POOL_WIDTH), lambda b, s: (b * nblk + s, 0)),
        out_shape=jax.ShapeDtypeStruct((batch * seq, POOL_WIDTH), BF16),
        scratch_shapes=[pltpu.VMEM((128, POOL_WIDTH), F32)],
        compiler_params=_cparams(("arbitrary", "arbitrary")),
        name="pool",
    )(proj, pw, pb, psc)


def _lru_kernel(xr_ref, gr_ref, cw_ref, cb_ref, wa_ref, ba_ref, wx_ref, bx_ref, lam_ref,
                o_ref, ext_ref, hc_ref):
    s = pl.program_id(1)
    tl = xr_ref.shape[0]

    @pl.when(s == 0)
    def _():
        ext_ref[0:HALO, :] = jnp.zeros((HALO, LRU_WIDTH), F32)
        hc_ref[...] = jnp.zeros_like(hc_ref)

    @pl.when(s > 0)
    def _():
        ext_ref[0:HALO, :] = ext_ref[tl:tl + HALO, :]

    ext_ref[HALO:HALO + tl, :] = xr_ref[...]

    row = lax.broadcasted_iota(jnp.int32, (tl, 128), 0)
    first = (row == 0) & (s == 0)
    nsp = -LRU_C * _softplus(-lam_ref[...])
    cw = cw_ref[...]

    for blk in range(LRU_BLOCKS):
        sl = slice(blk * 128, (blk + 1) * 128)
        xc = _conv_from_ext(ext_ref, blk * 128, 128, cw[:, sl], LRU_CONV, tl) + cb_ref[:, sl]
        xb = xc.astype(BF16)
        r = _sigmoid(_dot(xb, wa_ref[blk]) + ba_ref[:, sl])
        i = _sigmoid(_dot(xb, wx_ref[blk]) + bx_ref[:, sl])
        log_a = r * nsp[:, sl]
        a = jnp.exp(log_a)
        mult = jnp.sqrt(-jnp.tanh(log_a) * (a * a + 1.0))
        mult = jnp.where(first, 1.0, mult)
        bv = mult * i * xc
        sh = 1
        while sh < tl:
            keep = row >= sh
            a_sh = jnp.where(keep, pltpu.roll(a, sh, 0), 1.0)
            b_sh = jnp.where(keep, pltpu.roll(bv, sh, 0), 0.0)
            bv = a * b_sh + bv
            a = a * a_sh
            sh *= 2
        h = a * hc_ref[:, sl] + bv
        hc_ref[:, sl] = h[tl - 1:tl, :]
        o_ref[:, sl] = (h * _gelu(gr_ref[:, sl])).astype(o_ref.dtype)


def _lru(proj, cw, cb, wa, ba, wx, bx, lam, *, batch, seq, tl):
    nblk = seq // tl
    vec = pl.BlockSpec((1, LRU_WIDTH), lambda b, s: (0, 0))
    mat = pl.BlockSpec((LRU_BLOCKS, 128, 128), lambda b, s: (0, 0, 0))
    return pl.pallas_call(
        _lru_kernel,
        grid=(batch, nblk),
        in_specs=[
            pl.BlockSpec((tl, LRU_WIDTH), lambda b, s: (b * nblk + s, COL_XR // LRU_WIDTH)),
            pl.BlockSpec((tl, LRU_WIDTH), lambda b, s: (b * nblk + s, COL_GR // LRU_WIDTH)),
            pl.BlockSpec((LRU_CONV, LRU_WIDTH), lambda b, s: (0, 0)),
            vec, mat, vec, mat, vec, vec,
        ],
        out_specs=pl.BlockSpec((tl, LRU_WIDTH), lambda b, s: (b * nblk + s, 0)),
        out_shape=jax.ShapeDtypeStruct((batch * seq, LRU_WIDTH), BF16),
        scratch_shapes=[pltpu.VMEM((tl + HALO, LRU_WIDTH), F32), pltpu.VMEM((1, LRU_WIDTH), F32)],
        compiler_params=_cparams(("arbitrary", "arbitrary")),
        name="lru",
    )(proj, proj, cw, cb, wa, ba, wx, bx, lam)


def _unit_lower_inverses(a_list, eye, level_masks):
    m8 = level_masks[0]
    zero = jnp.zeros((), BF16)
    ab = [a.astype(BF16) for a in a_list]
    n1 = [jnp.where(m8, -a, zero) for a in ab]
    p = [eye + n.astype(F32) for n in n1]
    n2 = [_dot(n, n).astype(BF16) for n in n1]
    p = [pi + _dot(pi.astype(BF16), n) for pi, n in zip(p, n2)]
    n4 = [_dot(n, n).astype(BF16) for n in n2]
    p = [pi + _dot(pi.astype(BF16), n) for pi, n in zip(p, n4)]
    for lvl in range(len(level_masks)):
        inner = level_masks[lvl]
        sel = ~inner if lvl + 1 == len(level_masks) else (level_masks[lvl + 1] & ~inner)
        pb = [pi.astype(BF16) for pi in p]
        t = [_dot(pbi, jnp.where(sel, a, zero)).astype(BF16) for pbi, a in zip(pb, ab)]
        p = [pi - _dot(ti, pbi) for pi, ti, pbi in zip(p, t, pb)]
    return p


def _gdn_kernel(q_ref, k_ref, v_ref, z_ref, ab_ref, cw_ref, alog_ref, dtb_ref, nw_ref,
                o_ref, ext_ref, qs_ref, ks_ref, vs_ref, gc_ref, beta_ref, gt_ref, eg_ref, ekd_ref,
                attn_ref, qd_ref, kd_ref, rk_ref, rv_ref, w_ref, u_ref, os_ref, st_ref):
    s = pl.program_id(1)
    ts = q_ref.shape[0]

    @pl.when(s == 0)
    def _():
        ext_ref[0:HALO, :] = jnp.zeros((HALO, 3 * GDN_WIDTH), F32)
        st_ref[...] = jnp.zeros_like(st_ref)

    @pl.when(s > 0)
    def _():
        ext_ref[0:HALO, :] = ext_ref[ts:ts + HALO, :]

    ext_ref[HALO:HALO + ts, 0:GDN_WIDTH] = q_ref[...]
    ext_ref[HALO:HALO + ts, GDN_WIDTH:2 * GDN_WIDTH] = k_ref[...]
    ext_ref[HALO:HALO + ts, 2 * GDN_WIDTH:3 * GDN_WIDTH] = v_ref[...]

    cw = cw_ref[...]
    for h in range(GDN_HEADS):
        for ti, dst in enumerate((qs_ref, ks_ref, vs_ref)):
            c0 = ti * GDN_WIDTH + h * HEAD_DIM
            t = _conv_from_ext(ext_ref, c0, HEAD_DIM, cw[:, c0:c0 + HEAD_DIM], GDN_CONV, ts)
            t = t * _sigmoid(t)
            if ti < 2:
                t = t * lax.rsqrt(jnp.sum(t * t, axis=-1, keepdims=True) + EPS)
            if ti == 0:
                t = t * (HEAD_DIM ** -0.5)
            dst[:, h * HEAD_DIM:(h + 1) * HEAD_DIM] = t

    ab = ab_ref[...]
    graw = -jnp.exp(alog_ref[...]) * _softplus(ab + dtb_ref[...])
    beta_ref[...] = _sigmoid(ab)
    rr = lax.broadcasted_iota(jnp.int32, (ts, ts), 0)
    cc = lax.broadcasted_iota(jnp.int32, (ts, ts), 1)
    ltri = ((rr >= cc) & ((rr >> CHUNK_SHIFT) == (cc >> CHUNK_SHIFT))).astype(BF16)
    g1 = graw.astype(BF16)
    r1 = graw - g1.astype(F32)
    g2 = r1.astype(BF16)
    g3 = (r1 - g2.astype(F32)).astype(BF16)
    gc_ref[...] = _dot(ltri, g1) + _dot(ltri, g2) + _dot(ltri, g3)

    ri = lax.broadcasted_iota(jnp.int32, (CHUNK, CHUNK), 0)
    ci = lax.broadcasted_iota(jnp.int32, (CHUNK, CHUNK), 1)
    causal = ri >= ci
    strict = ri > ci
    eye = (ri == ci).astype(F32)
    level_masks = tuple((ri >> sh) == (ci >> sh) for sh in (3, 4, 5, 6))
    nchunk = ts // CHUNK
    probs = [(c, h) for c in range(nchunk) for h in range(GDN_HEADS)]

    def rows_of(c):
        return slice(c * CHUNK, (c + 1) * CHUNK)

    def lanes_of(h):
        return slice(h * HEAD_DIM, (h + 1) * HEAD_DIM)

    egl = []
    for c in range(nchunk):
        gcb = gc_ref[rows_of(c), :]
        gt_ref[rows_of(c), :] = gcb.T
        gl = gcb[CHUNK - 1:CHUNK, :]
        eg_ref[rows_of(c), :] = jnp.exp(gcb)
        ekd_ref[rows_of(c), :] = jnp.exp(gl - gcb)
        egl.append(jnp.exp(gl))

    a_list = []
    for c, h in probs:
        rows, sl = rows_of(c), lanes_of(h)
        qh = qs_ref[rows, sl]
        kh = ks_ref[rows, sl]
        vh = vs_ref[rows, sl]
        gcol = gc_ref[rows, :][:, h:h + 1]
        grow = gt_ref[c * CHUNK + h:c * CHUNK + h + 1, :]
        dec = jnp.exp(jnp.where(causal, gcol - grow, NEG_BIG))
        bcol = beta_ref[rows, :][:, GDN_HEADS + h:GDN_HEADS + h + 1]
        egc = eg_ref[rows, :][:, h:h + 1]
        ekc = ekd_ref[rows, :][:, h:h + 1]
        kb = kh.astype(BF16)
        a_list.append(jnp.where(strict, bcol * _dot_nt(kb, kb) * dec, 0.0))
        attn_ref[rows, sl] = (_dot_nt(qh.astype(BF16), kb) * dec).astype(BF16)
        qd_ref[rows, sl] = (qh * egc).astype(BF16)
        kd_ref[rows, sl] = (kh * ekc).astype(BF16)
        rk_ref[rows, sl] = (kh * (bcol * egc)).astype(BF16)
        rv_ref[rows, sl] = (vh * bcol).astype(BF16)

    minv = _unit_lower_inverses(a_list, eye, level_masks)
    for (c, h), mi in zip(probs, minv):
        rows, sl = rows_of(c), lanes_of(h)
        mb = mi.astype(BF16)
        w_ref[rows, sl] = _dot(mb, rk_ref[rows, sl]).astype(BF16)
        u_ref[rows, sl] = _dot(mb, rv_ref[rows, sl])

    for c in range(nchunk):
        rows = rows_of(c)
        st = [st_ref[h] for h in range(GDN_HEADS)]
        sb = [x.astype(BF16) for x in st]
        vn = [u_ref[rows, lanes_of(h)] - _dot(w_ref[rows, lanes_of(h)], sb[h]) for h in range(GDN_HEADS)]
        vnb = [x.astype(BF16) for x in vn]
        for h in range(GDN_HEADS):
            sl = lanes_of(h)
            st_ref[h] = st[h] * egl[c][:, h:h + 1] + _dot_tn(kd_ref[rows, sl], vnb[h])
            os_ref[rows, sl] = _dot(qd_ref[rows, sl], sb[h]) + _dot(attn_ref[rows, sl], vnb[h])

    nw = nw_ref[...]
    for h in range(GDN_HEADS):
        sl = slice(h * HEAD_DIM, (h + 1) * HEAD_DIM)
        o = os_ref[:, sl]
        zz = z_ref[:, sl]
        y = o * lax.rsqrt(jnp.mean(o * o, axis=-1, keepdims=True) + EPS) * nw * (zz * _sigmoid(zz))
        o_ref[:, sl] = y.astype(o_ref.dtype)


def _gdn(proj, cw, alog, dtb, nw, *, batch, seq, ts):
    nblk = seq // ts

    def col(c0, width):
        return pl.BlockSpec((ts, width), lambda b, s: (b * nblk + s, c0 // width))

    small = pl.BlockSpec((1, 128), lambda b, s: (0, 0))
    return pl.pallas_call(
        _gdn_kernel,
        grid=(batch, nblk),
        in_specs=[
            col(COL_Q, GDN_WIDTH), col(COL_K, GDN_WIDTH), col(COL_V, GDN_WIDTH), col(COL_Z, GDN_WIDTH),
            col(COL_AB, 128),
            pl.BlockSpec((GDN_CONV, 3 * GDN_WIDTH), lambda b, s: (0, 0)),
            small, small, small,
        ],
        out_specs=pl.BlockSpec((ts, GDN_WIDTH), lambda b, s: (b * nblk + s, 0)),
        out_shape=jax.ShapeDtypeStruct((batch * seq, GDN_WIDTH), BF16),
        scratch_shapes=(
            [pltpu.VMEM((ts + HALO, 3 * GDN_WIDTH), F32)]
            + [pltpu.VMEM((ts, GDN_WIDTH), F32)] * 3
            + [pltpu.VMEM((ts, 128), F32)] * 5
            + [pltpu.VMEM((ts, GDN_WIDTH), BF16)] * 6
            + [pltpu.VMEM((ts, GDN_WIDTH), F32)] * 2
            + [pltpu.VMEM((GDN_HEADS, HEAD_DIM, HEAD_DIM), F32)]
        ),
        compiler_params=_cparams(("arbitrary", "arbitrary")),
        name="gdn",
    )(proj, proj, proj, proj, proj, cw, alog, dtb, nw)


def _outproj_kernel(x_ref, yp_ref, yg_ref, yl_ref, wp_ref, wg_ref, wl_ref, o_ref):
    acc = _dot(yp_ref[...], wp_ref[...]) + _dot(yg_ref[...], wg_ref[...]) + _dot(yl_ref[...], wl_ref[...])
    o_ref[...] = x_ref[...] + acc


def _outproj(x, yp, yg, yl, wp, wg, wl, *, tm):
    t, d = x.shape

    def rows(width):
        return pl.BlockSpec((tm, width), lambda i: (i, 0))

    def whole(width):
        return pl.BlockSpec((width, d), lambda i: (0, 0))

    return pl.pallas_call(
        _outproj_kernel,
        grid=(t // tm,),
        in_specs=[rows(d), rows(POOL_WIDTH), rows(GDN_WIDTH), rows(LRU_WIDTH),
                  whole(POOL_WIDTH), whole(GDN_WIDTH), whole(LRU_WIDTH)],
        out_specs=rows(d),
        out_shape=jax.ShapeDtypeStruct((t, d), F32),
        compiler_params=_cparams(("arbitrary",)),
        name="outproj",
    )(x, yp, yg, yl, wp, wg, wl)


def _ffn_kernel(x_ref, nw_ref, wg_ref, wv_ref, wd_ref, cw_ref, fnw_ref, o_ref,
                h_ref, acc_ref, ext_ref, carry_ref, *, tiles_per_seq, final_norm):
    i = pl.program_id(0)
    f = pl.program_id(1)
    tm = x_ref.shape[0]

    @pl.when(f == 0)
    def _():
        h_ref[...] = _rmsnorm(x_ref[...], nw_ref[...]).astype(BF16)

    hb = h_ref[...]
    gate = _dot(hb, wg_ref[...])
    val = _dot(hb, wv_ref[...])

    seq_start = (i % tiles_per_seq) == 0

    @pl.when(seq_start)
    def _():
        ext_ref[0:HALO, :] = jnp.zeros((HALO, gate.shape[1]), F32)

    @pl.when(jnp.logical_not(seq_start))
    def _():
        ext_ref[0:HALO, :] = carry_ref[f]

    ext_ref[HALO:HALO + tm, :] = gate
    carry_ref[f] = gate[tm - HALO:tm, :]
    cw = cw_ref[...]
    conv = _conv_from_ext(ext_ref, 0, gate.shape[1], cw, FFN_CONV, tm)
    act = (_gelu(conv) * val).astype(BF16)
    contrib = _dot(act, wd_ref[...])

    @pl.when(f == 0)
    def _():
        acc_ref[...] = contrib

    @pl.when(f > 0)
    def _():
        acc_ref[...] += contrib

    @pl.when(f == pl.num_programs(1) - 1)
    def _():
        y = x_ref[...] + acc_ref[...]
        if final_norm:
            y = _rmsnorm(y, fnw_ref[...])
        o_ref[...] = y


def _ffn(x, nw, w_up, w_down, cw, fnw, *, seq, tm, tf, final_norm):
    t, d = x.shape
    nf = D_FF // tf
    kern = functools.partial(_ffn_kernel, tiles_per_seq=seq // tm, final_norm=final_norm)
    return pl.pallas_call(
        kern,
        grid=(t // tm, nf),
        in_specs=[
            pl.BlockSpec((tm, d), lambda i, f: (i, 0)),
            pl.BlockSpec((1, d), lambda i, f: (0, 0)),
            pl.BlockSpec((d, tf), lambda i, f: (0, f)),
            pl.BlockSpec((d, tf), lambda i, f: (0, f + nf)),
            pl.BlockSpec((tf, d), lambda i, f: (f, 0)),
            pl.BlockSpec((FFN_CONV, tf), lambda i, f: (0, f)),
            pl.BlockSpec((1, d), lambda i, f: (0, 0)),
        ],
        out_specs=pl.BlockSpec((tm, d), lambda i, f: (i, 0)),
        out_shape=jax.ShapeDtypeStruct((t, d), F32),
        scratch_shapes=[
            pltpu.VMEM((tm, d), BF16),
            pltpu.VMEM((tm, d), F32),
            pltpu.VMEM((tm + HALO, tf), F32),
            pltpu.VMEM((nf, HALO, tf), F32),
        ],
        compiler_params=_cparams(("arbitrary", "arbitrary")),
        name="ffn",
    )(x, nw, w_up, w_up, w_down, cw, fnw)


def _pad_cols(w, width):
    return jnp.pad(w, ((0, 0), (0, width - w.shape[1])))


def _reorder_w_in(w):
    gdn0 = POOL_WIDTH
    ab0 = gdn0 + 4 * GDN_WIDTH
    lru0 = ab0 + 2 * GDN_HEADS
    wb = w.astype(BF16)
    cols = [wb[:, gdn0:ab0], wb[:, lru0:lru0 + 2 * LRU_WIDTH], wb[:, :POOL_WIDTH], wb[:, ab0:lru0]]
    used = sum(c.shape[1] for c in cols)
    return jnp.concatenate(cols + [jnp.zeros((w.shape[0], PROJ_COLS - used), BF16)], axis=1)


def _row(v, width=None):
    v = v.reshape(1, -1).astype(F32)
    return v if width is None else _pad_cols(v, width)


def kernel(x, norm1_w, w_in, pool_w, pool_b, pool_scale, gdn_conv_w, gdn_a_log, gdn_dt_bias, gdn_norm_w,
           lru_conv_w, lru_conv_b, lru_wa, lru_ba, lru_wx, lru_bx, lru_lambda, w_out, norm2_w, ffn_up,
           ffn_conv_w, ffn_down, final_norm_w):
    batch, seq, d = x.shape
    xf = x.reshape(batch * seq, d)
    for l in range(DEPTH):
        proj = _inproj(xf, _row(norm1_w[l]), _reorder_w_in(w_in[l]), tm=1024, tn=768)
        y_pool = _pool(proj, pool_w[l].astype(BF16), _row(pool_b[l]), _row(pool_scale[l]),
                       batch=batch, seq=seq, tp=256)
        y_gdn = _gdn(proj, gdn_conv_w[l], _row(gdn_a_log[l], 128), _row(gdn_dt_bias[l], 128),
                     _row(gdn_norm_w[l]), batch=batch, seq=seq, ts=2 * CHUNK)
        y_lru = _lru(proj, lru_conv_w[l], _row(lru_conv_b[l]), lru_wa[l].astype(BF16), _row(lru_ba[l]),
                     lru_wx[l].astype(BF16), _row(lru_bx[l]), _row(lru_lambda[l]),
                     batch=batch, seq=seq, tl=256)
        wo = w_out[l].astype(BF16)
        x1 = _outproj(xf, y_pool, y_gdn, y_lru, wo[:POOL_WIDTH], wo[POOL_WIDTH:POOL_WIDTH + GDN_WIDTH],
                      wo[POOL_WIDTH + GDN_WIDTH:], tm=512)
        xf = _ffn(x1, _row(norm2_w[l]), ffn_up[l].astype(BF16), ffn_down[l].astype(BF16), ffn_conv_w[l],
                  _row(final_norm_w), seq=seq, tm=512, tf=512, final_norm=(l == DEPTH - 1))
    return xf.reshape(batch, seq, d)
```

```python
import functools

import jax
import jax.numpy as jnp
from jax import lax
from jax.experimental import pallas as pl
from jax.experimental.pallas import tpu as pltpu

F32 = jnp.float32
BF16 = jnp.bfloat16

D_MODEL = 2048
DEPTH = 2
POOL_WINDOWS = (2, 4, 8, 16)
POOL_GROUP_DIM = 128
POOL_WIDTH = 512
GDN_HEADS = 6
HEAD_DIM = 128
GDN_WIDTH = 768
GDN_CONV = 4
GDN_CHUNK = 64
LRU_BLOCKS = 6
LRU_WIDTH = 768
LRU_CONV = 4
LRU_C = 8.0
D_FF = 3 * D_MODEL
FFN_CONV = 3
EPS = 1e-6

COL_Q, COL_K, COL_V, COL_Z = 0, 768, 1536, 2304
COL_XR, COL_GR = 3072, 3840
COL_POOL = 4608
COL_AB = 5120
PROJ_COLS = 5376

HALO = 8
CHUNK_SHIFT = 7
CHUNK = 1 << CHUNK_SHIFT
NEG_BIG = -1e30

V7X_VMEM_LIMIT = 56 * 1024 * 1024


def _cparams(sem):
    return pltpu.CompilerParams(dimension_semantics=sem, vmem_limit_bytes=V7X_VMEM_LIMIT)


def _dot(a, b):
    return jnp.dot(a, b, preferred_element_type=F32)


def _dot_nt(a, b):
    return lax.dot_general(a, b, (((1,), (1,)), ((), ())), preferred_element_type=F32)


def _dot_tn(a, b):
    return lax.dot_general(a, b, (((0,), (0,)), ((), ())), preferred_element_type=F32)


def _sigmoid(x):
    return 0.5 + 0.5 * jnp.tanh(0.5 * x)


def _softplus(x):
    return jnp.maximum(x, 0.0) + jnp.log1p(jnp.exp(-jnp.abs(x)))


def _gelu(x):
    return jax.nn.gelu(x, approximate=True)


def _rmsnorm(x, w):
    return x * lax.rsqrt(jnp.mean(x * x, axis=-1, keepdims=True) + EPS) * w


def _conv_from_ext(ext_ref, col0, width, w, taps, rows):
    acc = None
    for j in range(taps):
        off = HALO - (taps - 1) + j
        term = ext_ref[pl.ds(off, rows), col0:col0 + width] * w[j:j + 1, :]
        acc = term if acc is None else acc + term
    return acc


def _inproj_kernel(x_ref, nw_ref, w_ref, o_ref, h_ref):
    @pl.when(pl.program_id(1) == 0)
    def _():
        h_ref[...] = _rmsnorm(x_ref[...], nw_ref[...]).astype(BF16)

    o_ref[...] = _dot_nt(h_ref[...], w_ref[...])


def _inproj(x, nw, w, layer, *, tm, tn):
    t, d = x.shape
    n = w.shape[1]
    return pl.pallas_call(
        _inproj_kernel,
        grid=(t // tm, n // tn),
        in_specs=[
            pl.BlockSpec((tm, d), lambda i, j: (i, 0)),
            pl.BlockSpec((1, d), lambda i, j: (0, 0)),
            pl.BlockSpec((None, tn, d), lambda i, j: (layer, j, 0)),
        ],
        out_specs=pl.BlockSpec((tm, tn), lambda i, j: (i, j)),
        out_shape=jax.ShapeDtypeStruct((t, n), F32),
        scratch_shapes=[pltpu.VMEM((tm, d), BF16)],
        compiler_params=_cparams(("arbitrary", "arbitrary")),
        name="inproj",
    )(x, nw, w)


def _pool_kernel(u_ref, w_ref, b_ref, sc_ref, o_ref, prev_ref):
    s = pl.program_id(1)
    tp = u_ref.shape[0]

    @pl.when(s == 0)
    def _():
        prev_ref[...] = jnp.zeros_like(prev_ref)

    r = lax.broadcasted_iota(jnp.int32, (tp, tp), 0)
    c = lax.broadcasted_iota(jnp.int32, (tp, tp), 1)
    lag = r - c
    rh = lax.broadcasted_iota(jnp.int32, (128, 128), 0)
    ch = lax.broadcasted_iota(jnp.int32, (128, 128), 1)
    lag_h = rh + 128 - ch
    pos = s * tp + lax.broadcasted_iota(jnp.int32, (tp, 1), 0)

    def split3(v):
        v1 = v.astype(BF16)
        r1 = v - v1.astype(F32)
        v2 = r1.astype(BF16)
        v3 = (r1 - v2.astype(F32)).astype(BF16)
        return v1, v2, v3

    for g, win in enumerate(POOL_WINDOWS):
        sl = slice(g * POOL_GROUP_DIM, (g + 1) * POOL_GROUP_DIM)
        u = u_ref[:, sl]
        up = prev_ref[:, sl]
        band = ((lag >= 0) & (lag < win)).astype(BF16)
        band_h = (lag_h < win).astype(BF16)
        u1, u2, u3 = split3(u)
        p1, p2, p3 = split3(up)
        wsum = _dot(band, u1) + _dot(band, u2) + _dot(band, u3)
        head = _dot(band_h, p1) + _dot(band_h, p2) + _dot(band_h, p3)
        wsum = jnp.concatenate([wsum[:128] + head, wsum[128:]], axis=0)
        cnt = jnp.minimum(pos + 1, win).astype(F32)
        d = wsum / cnt - u
        y = _dot(d.astype(BF16), w_ref[g]) + b_ref[:, sl]
        o_ref[:, sl] = (y * sc_ref[:, sl]).astype(o_ref.dtype)

    prev_ref[...] = u_ref[tp - 128:tp, :]


def _pool(proj, pw, pb, psc, *, batch, seq, tp):
    nblk = seq // tp
    return pl.pallas_call(
        _pool_kernel,
        grid=(batch, nblk),
        in_specs=[
            pl.BlockSpec((tp, POOL_WIDTH), lambda b, s: (b * nblk + s, COL_POOL // POOL_WIDTH)),
            pl.BlockSpec((4, 128, 128), lambda b, s: (0, 0, 0)),
            pl.BlockSpec((1, POOL_WIDTH), lambda b, s: (0, 0)),
            pl.BlockSpec((1, POOL_WIDTH), lambda b, s: (0, 0)),
        ],
        out_specs=pl.BlockSpec((tp, POOL_WIDTH), lambda b, s: (b * nblk + s, 0)),
        out_shape=jax.ShapeDtypeStruct((batch * seq, POOL_WIDTH), BF16),
        scratch_shapes=[pltpu.VMEM((128, POOL_WIDTH), F32)],
        compiler_params=_cparams(("arbitrary", "arbitrary")),
        name="pool",
    )(proj, pw, pb, psc)


def _lru_kernel(xr_ref, gr_ref, cw_ref, cb_ref, wa_ref, ba_ref, wx_ref, bx_ref, lam_ref,
                o_ref, ext_ref, hc_ref):
    s = pl.program_id(1)
    tl = xr_ref.shape[0]

    @pl.when(s == 0)
    def _():
        ext_ref[0:HALO, :] = jnp.zeros((HALO, LRU_WIDTH), F32)
        hc_ref[...] = jnp.zeros_like(hc_ref)

    @pl.when(s > 0)
    def _():
        ext_ref[0:HALO, :] = ext_ref[tl:tl + HALO, :]

    ext_ref[HALO:HALO + tl, :] = xr_ref[...]

    row = lax.broadcasted_iota(jnp.int32, (tl, 128), 0)
    first = (row == 0) & (s == 0)
    nsp = -LRU_C * _softplus(-lam_ref[...])
    cw = cw_ref[...]

    for blk in range(LRU_BLOCKS):
        sl = slice(blk * 128, (blk + 1) * 128)
        xc = _conv_from_ext(ext_ref, blk * 128, 128, cw[:, sl], LRU_CONV, tl) + cb_ref[:, sl]
        xb = xc.astype(BF16)
        r = _sigmoid(_dot(xb, wa_ref[blk]) + ba_ref[:, sl])
        i = _sigmoid(_dot(xb, wx_ref[blk]) + bx_ref[:, sl])
        log_a = r * nsp[:, sl]
        a = jnp.exp(log_a)
        mult = jnp.sqrt(-jnp.tanh(log_a) * (a * a + 1.0))
        mult = jnp.where(first, 1.0, mult)
        bv = mult * i * xc
        sh = 1
        while sh < tl:
            keep = row >= sh
            a_sh = jnp.where(keep, pltpu.roll(a, sh, 0), 1.0)
            b_sh = jnp.where(keep, pltpu.roll(bv, sh, 0), 0.0)
            bv = a * b_sh + bv
            a = a * a_sh
            sh *= 2
        h = a * hc_ref[:, sl] + bv
        hc_ref[:, sl] = h[tl - 1:tl, :]
        o_ref[:, sl] = (h * _gelu(gr_ref[:, sl])).astype(o_ref.dtype)


def _lru(proj, cw, cb, wa, ba, wx, bx, lam, *, batch, seq, tl):
    nblk = seq // tl
    vec = pl.BlockSpec((1, LRU_WIDTH), lambda b, s: (0, 0))
    mat = pl.BlockSpec((LRU_BLOCKS, 128, 128), lambda b, s: (0, 0, 0))
    return pl.pallas_call(
        _lru_kernel,
        grid=(batch, nblk),
        in_specs=[
            pl.BlockSpec((tl, LRU_WIDTH), lambda b, s: (b * nblk + s, COL_XR // LRU_WIDTH)),
            pl.BlockSpec((tl, LRU_WIDTH), lambda b, s: (b * nblk + s, COL_GR // LRU_WIDTH)),
            pl.BlockSpec((LRU_CONV, LRU_WIDTH), lambda b, s: (0, 0)),
            vec, mat, vec, mat, vec, vec,
        ],
        out_specs=pl.BlockSpec((tl, LRU_WIDTH), lambda b, s: (b * nblk + s, 0)),
        out_shape=jax.ShapeDtypeStruct((batch * seq, LRU_WIDTH), BF16),
        scratch_shapes=[pltpu.VMEM((tl + HALO, LRU_WIDTH), F32), pltpu.VMEM((1, LRU_WIDTH), F32)],
        compiler_params=_cparams(("arbitrary", "arbitrary")),
        name="lru",
    )(proj, proj, cw, cb, wa, ba, wx, bx, lam)


def _unit_lower_inverses(a_list, eye, level_masks):
    m8 = level_masks[0]
    zero = jnp.zeros((), BF16)
    ab = [a.astype(BF16) for a in a_list]
    n1 = [jnp.where(m8, -a, zero) for a in ab]
    p = [eye + n.astype(F32) for n in n1]
    n2 = [_dot(n, n).astype(BF16) for n in n1]
    p = [pi + _dot(pi.astype(BF16), n) for pi, n in zip(p, n2)]
    n4 = [_dot(n, n).astype(BF16) for n in n2]
    p = [pi + _dot(pi.astype(BF16), n) for pi, n in zip(p, n4)]
    for lvl in range(len(level_masks)):
        inner = level_masks[lvl]
        sel = ~inner if lvl + 1 == len(level_masks) else (level_masks[lvl + 1] & ~inner)
        pb = [pi.astype(BF16) for pi in p]
        t = [_dot(pbi, jnp.where(sel, a, zero)).astype(BF16) for pbi, a in zip(pb, ab)]
        p = [pi - _dot(ti, pbi) for pi, ti, pbi in zip(p, t, pb)]
    return p


def _gdn_kernel(q_ref, k_ref, v_ref, z_ref, ab_ref, cw_ref, alog_ref, dtb_ref, nw_ref,
                o_ref, ext_ref, qs_ref, ks_ref, vs_ref, gc_ref, beta_ref, gt_ref, eg_ref, ekd_ref,
                attn_ref, qd_ref, kd_ref, rk_ref, rv_ref, w_ref, u_ref, os_ref, st_ref):
    s = pl.program_id(1)
    ts = q_ref.shape[0]

    @pl.when(s == 0)
    def _():
        ext_ref[0:HALO, :] = jnp.zeros((HALO, 3 * GDN_WIDTH), F32)
        st_ref[...] = jnp.zeros_like(st_ref)

    @pl.when(s > 0)
    def _():
        ext_ref[0:HALO, :] = ext_ref[ts:ts + HALO, :]

    ext_ref[HALO:HALO + ts, 0:GDN_WIDTH] = q_ref[...]
    ext_ref[HALO:HALO + ts, GDN_WIDTH:2 * GDN_WIDTH] = k_ref[...]
    ext_ref[HALO:HALO + ts, 2 * GDN_WIDTH:3 * GDN_WIDTH] = v_ref[...]

    cw = cw_ref[...]
    for h in range(GDN_HEADS):
        for ti, dst in enumerate((qs_ref, ks_ref, vs_ref)):
            c0 = ti * GDN_WIDTH + h * HEAD_DIM
            t = _conv_from_ext(ext_ref, c0, HEAD_DIM, cw[:, c0:c0 + HEAD_DIM], GDN_CONV, ts)
            t = t * _sigmoid(t)
            if ti < 2:
                t = t * lax.rsqrt(jnp.sum(t * t, axis=-1, keepdims=True) + EPS)
            if ti == 0:
                t = t * (HEAD_DIM ** -0.5)
            dst[:, h * HEAD_DIM:(h + 1) * HEAD_DIM] = t

    ab = ab_ref[...]
    graw = -jnp.exp(alog_ref[...]) * _softplus(ab + dtb_ref[...])
    beta_ref[...] = _sigmoid(ab)
    rr = lax.broadcasted_iota(jnp.int32, (ts, ts), 0)
    cc = lax.broadcasted_iota(jnp.int32, (ts, ts), 1)
    ltri = ((rr >= cc) & ((rr >> CHUNK_SHIFT) == (cc >> CHUNK_SHIFT))).astype(BF16)
    g1 = graw.astype(BF16)
    r1 = graw - g1.astype(F32)
    g2 = r1.astype(BF16)
    g3 = (r1 - g2.astype(F32)).astype(BF16)
    gc_ref[...] = _dot(ltri, g1) + _dot(ltri, g2) + _dot(ltri, g3)

    ri = lax.broadcasted_iota(jnp.int32, (CHUNK, CHUNK), 0)
    ci = lax.broadcasted_iota(jnp.int32, (CHUNK, CHUNK), 1)
    causal = ri >= ci
    strict = ri > ci
    eye = (ri == ci).astype(F32)
    level_masks = tuple((ri >> sh) == (ci >> sh) for sh in (3, 4, 5, 6))
    nchunk = ts // CHUNK
    probs = [(c, h) for c in range(nchunk) for h in range(GDN_HEADS)]

    def rows_of(c):
        return slice(c * CHUNK, (c + 1) * CHUNK)

    def lanes_of(h):
        return slice(h * HEAD_DIM, (h + 1) * HEAD_DIM)

    egl = []
    for c in range(nchunk):
        gcb = gc_ref[rows_of(c), :]
        gt_ref[rows_of(c), :] = gcb.T
        gl = gcb[CHUNK - 1:CHUNK, :]
        eg_ref[rows_of(c), :] = jnp.exp(gcb)
        ekd_ref[rows_of(c), :] = jnp.exp(gl - gcb)
        egl.append(jnp.exp(gl))

    a_list = []
    for c, h in probs:
        rows, sl = rows_of(c), lanes_of(h)
        qh = qs_ref[rows, sl]
        kh = ks_ref[rows, sl]
        vh = vs_ref[rows, sl]
        gcol = gc_ref[rows, :][:, h:h + 1]
        grow = gt_ref[c * CHUNK + h:c * CHUNK + h + 1, :]
        dec = jnp.exp(jnp.where(causal, gcol - grow, NEG_BIG))
        bcol = beta_ref[rows, :][:, GDN_HEADS + h:GDN_HEADS + h + 1]
        egc = eg_ref[rows, :][:, h:h + 1]
        ekc = ekd_ref[rows, :][:, h:h + 1]
        kb = kh.astype(BF16)
        a_list.append(jnp.where(strict, bcol * _dot_nt(kb, kb) * dec, 0.0))
        attn_ref[rows, sl] = (_dot_nt(qh.astype(BF16), kb) * dec).astype(BF16)
        qd_ref[rows, sl] = (qh * egc).astype(BF16)
        kd_ref[rows, sl] = (kh * ekc).astype(BF16)
        rk_ref[rows, sl] = (kh * (bcol * egc)).astype(BF16)
        rv_ref[rows, sl] = (vh * bcol).astype(BF16)

    minv = _unit_lower_inverses(a_list, eye, level_masks)
    for (c, h), mi in zip(probs, minv):
        rows, sl = rows_of(c), lanes_of(h)
        mb = mi.astype(BF16)
        w_ref[rows, sl] = _dot(mb, rk_ref[rows, sl]).astype(BF16)
        u_ref[rows, sl] = _dot(mb, rv_ref[rows, sl])

    for c in range(nchunk):
        rows = rows_of(c)
        st = [st_ref[h] for h in range(GDN_HEADS)]
        sb = [x.astype(BF16) for x in st]
        vn = [u_ref[rows, lanes_of(h)] - _dot(w_ref[rows, lanes_of(h)], sb[h]) for h in range(GDN_HEADS)]
        vnb = [x.astype(BF16) for x in vn]
        for h in range(GDN_HEADS):
            sl = lanes_of(h)
            st_ref[h] = st[h] * egl[c][:, h:h + 1] + _dot_tn(kd_ref[rows, sl], vnb[h])
            os_ref[rows, sl] = _dot(qd_ref[rows, sl], sb[h]) + _dot(attn_ref[rows, sl], vnb[h])

    nw = nw_ref[...]
    for h in range(GDN_HEADS):
        sl = slice(h * HEAD_DIM, (h + 1) * HEAD_DIM)
        o = os_ref[:, sl]
        zz = z_ref[:, sl]
        y = o * lax.rsqrt(jnp.mean(o * o, axis=-1, keepdims=True) + EPS) * nw * (zz * _sigmoid(zz))
        o_ref[:, sl] = y.astype(o_ref.dtype)


def _gdn(proj, cw, alog, dtb, nw, *, batch, seq, ts):
    nblk = seq // ts

    def col(c0, width):
        return pl.BlockSpec((ts, width), lambda b, s: (b * nblk + s, c0 // width))

    small = pl.BlockSpec((1, 128), lambda b, s: (0, 0))
    return pl.pallas_call(
        _gdn_kernel,
        grid=(batch, nblk),
        in_specs=[
            col(COL_Q, GDN_WIDTH), col(COL_K, GDN_WIDTH), col(COL_V, GDN_WIDTH), col(COL_Z, GDN_WIDTH),
            col(COL_AB, 128),
            pl.BlockSpec((GDN_CONV, 3 * GDN_WIDTH), lambda b, s: (0, 0)),
            small, small, small,
        ],
        out_specs=pl.BlockSpec((ts, GDN_WIDTH), lambda b, s: (b * nblk + s, 0)),
        out_shape=jax.ShapeDtypeStruct((batch * seq, GDN_WIDTH), BF16),
        scratch_shapes=(
            [pltpu.VMEM((ts + HALO, 3 * GDN_WIDTH), F32)]
            + [pltpu.VMEM((ts, GDN_WIDTH), F32)] * 3
            + [pltpu.VMEM((ts, 128), F32)] * 5
            + [pltpu.VMEM((ts, GDN_WIDTH), BF16)] * 6
            + [pltpu.VMEM((ts, GDN_WIDTH), F32)] * 2
            + [pltpu.VMEM((GDN_HEADS, HEAD_DIM, HEAD_DIM), F32)]
        ),
        compiler_params=_cparams(("arbitrary", "arbitrary")),
        name="gdn",
    )(proj, proj, proj, proj, proj, cw, alog, dtb, nw)


def _outproj_kernel(x_ref, yp_ref, yg_ref, yl_ref, w_ref, o_ref):
    mixed = jnp.concatenate([yp_ref[...], yg_ref[...], yl_ref[...]], axis=1)
    o_ref[...] = x_ref[...] + _dot(mixed, w_ref[...])


def _outproj(x, yp, yg, yl, w, layer, *, tm):
    t, d = x.shape

    def rows(width):
        return pl.BlockSpec((tm, width), lambda i: (i, 0))

    return pl.pallas_call(
        _outproj_kernel,
        grid=(t // tm,),
        in_specs=[rows(d), rows(POOL_WIDTH), rows(GDN_WIDTH), rows(LRU_WIDTH),
                  pl.BlockSpec((None, d, d), lambda i: (layer, 0, 0))],
        out_specs=rows(d),
        out_shape=jax.ShapeDtypeStruct((t, d), F32),
        compiler_params=_cparams(("arbitrary",)),
        name="outproj",
    )(x, yp, yg, yl, w)


def _ffn_up_kernel(x_ref, nw_ref, wg_ref, wv_ref, cw_ref, o_ref, h_ref, g_ref, carry_ref,
                   *, tiles_per_seq, rb):
    i = pl.program_id(0)
    f = pl.program_id(1)
    tm = x_ref.shape[0]

    @pl.when(f == 0)
    def _():
        h_ref[...] = _rmsnorm(x_ref[...], nw_ref[...]).astype(BF16)

    halo = carry_ref[f]
    seq_start = (i % tiles_per_seq) == 0
    g_ref[0:HALO, :] = jnp.where(seq_start, jnp.zeros_like(halo), halo)
    cw = cw_ref[...]
    wg = wg_ref[...]
    wv = wv_ref[...]
    for r in range(tm // rb):
        hb = h_ref[r * rb:(r + 1) * rb, :]
        g_ref[HALO + r * rb:HALO + (r + 1) * rb, :] = _dot(hb, wg)
        val = _dot(hb, wv)
        conv = None
        for j in range(FFN_CONV):
            off = HALO - (FFN_CONV - 1) + j + r * rb
            term = g_ref[off:off + rb, :] * cw[j:j + 1, :]
            conv = term if conv is None else conv + term
        o_ref[r * rb:(r + 1) * rb, :] = (_gelu(conv) * val).astype(o_ref.dtype)
    carry_ref[f] = g_ref[tm:tm + HALO, :]


def _ffn_up(x, nw, w_up, cw, layer, *, seq, tm, tf, rb):
    t, d = x.shape
    nf = D_FF // tf
    kern = functools.partial(_ffn_up_kernel, tiles_per_seq=seq // tm, rb=rb)
    return pl.pallas_call(
        kern,
        grid=(t // tm, nf),
        in_specs=[
            pl.BlockSpec((tm, d), lambda i, f: (i, 0)),
            pl.BlockSpec((1, d), lambda i, f: (0, 0)),
            pl.BlockSpec((None, d, tf), lambda i, f: (layer, 0, f)),
            pl.BlockSpec((None, d, tf), lambda i, f: (layer, 0, nf + f)),
            pl.BlockSpec((None, FFN_CONV, tf), lambda i, f: (layer, 0, f)),
        ],
        out_specs=pl.BlockSpec((tm, tf), lambda i, f: (i, f)),
        out_shape=jax.ShapeDtypeStruct((t, D_FF), BF16),
        scratch_shapes=[
            pltpu.VMEM((tm, d), BF16),
            pltpu.VMEM((tm + HALO, tf), F32),
            pltpu.VMEM((nf, HALO, tf), F32),
        ],
        compiler_params=_cparams(("arbitrary", "arbitrary")),
        name="ffn_up",
    )(x, nw, w_up, w_up, cw)


def _ffn_down_kernel(a_ref, w_ref, x_ref, fnw_ref, o_ref, *, nj, final_norm):
    j = pl.program_id(1)
    tn = w_ref.shape[1]
    y = x_ref[...] + _dot(a_ref[...], w_ref[...])
    for jj in range(nj):
        @pl.when(j == jj)
        def _():
            o_ref[:, jj * tn:(jj + 1) * tn] = y

    if final_norm:
        @pl.when(j == nj - 1)
        def _():
            o_ref[...] = _rmsnorm(o_ref[...], fnw_ref[...])


def _ffn_down(act, w_down, x, fnw, layer, *, tm, tn, final_norm):
    t, d = x.shape
    nj = d // tn
    kern = functools.partial(_ffn_down_kernel, nj=nj, final_norm=final_norm)
    return pl.pallas_call(
        kern,
        grid=(t // tm, nj),
        in_specs=[
            pl.BlockSpec((tm, D_FF), lambda i, j: (i, 0)),
            pl.BlockSpec((None, D_FF, tn), lambda i, j: (layer, 0, j)),
            pl.BlockSpec((tm, tn), lambda i, j: (i, j)),
            pl.BlockSpec((1, d), lambda i, j: (0, 0)),
        ],
        out_specs=pl.BlockSpec((tm, d), lambda i, j: (i, 0)),
        out_shape=jax.ShapeDtypeStruct((t, d), F32),
        compiler_params=_cparams(("arbitrary", "arbitrary")),
        name="ffn_down",
    )(act, w_down, x, fnw)


def _pad_cols(w, width):
    return jnp.pad(w, ((0, 0), (0, width - w.shape[1])))


def _reorder_w_in(w):
    gdn0 = POOL_WIDTH
    ab0 = gdn0 + 4 * GDN_WIDTH
    lru0 = ab0 + 2 * GDN_HEADS
    wt = jnp.transpose(w, (2, 0, 1))
    rows = [wt[gdn0:ab0], wt[lru0:lru0 + 2 * LRU_WIDTH], wt[:POOL_WIDTH], wt[ab0:lru0]]
    used = sum(r.shape[0] for r in rows)
    rows.append(jnp.zeros((PROJ_COLS - used,) + wt.shape[1:], wt.dtype))
    return jnp.transpose(jnp.concatenate(rows, axis=0), (1, 0, 2)).astype(BF16)


TILES = dict(
    inproj=dict(tm=1024, tn=768),
    pool=dict(tp=256),
    gdn=dict(ts=2 * CHUNK),
    lru=dict(tl=256),
    outproj=dict(tm=512),
    ffn_up=dict(tm=1024, tf=512, rb=256),
    ffn_down=dict(tm=512, tn=1024),
)


def _row(v, width=None):
    v = v.reshape(1, -1).astype(F32)
    return v if width is None else _pad_cols(v, width)


def kernel(x, norm1_w, w_in, pool_w, pool_b, pool_scale, gdn_conv_w, gdn_a_log, gdn_dt_bias, gdn_norm_w,
           lru_conv_w, lru_conv_b, lru_wa, lru_ba, lru_wx, lru_bx, lru_lambda, w_out, norm2_w, ffn_up,
           ffn_conv_w, ffn_down, final_norm_w):
    batch, seq, d = x.shape
    xf = x.reshape(batch * seq, d)
    w_in_b = _reorder_w_in(w_in)
    w_out_b = w_out.astype(BF16)
    w_up_b = ffn_up.astype(BF16)
    w_down_b = ffn_down.astype(BF16)
    fnw = _row(final_norm_w)
    for l in range(DEPTH):
        proj = _inproj(xf, _row(norm1_w[l]), w_in_b, l, **TILES["inproj"])
        y_pool = _pool(proj, pool_w[l].astype(BF16), _row(pool_b[l]), _row(pool_scale[l]),
                       batch=batch, seq=seq, **TILES["pool"])
        y_gdn = _gdn(proj, gdn_conv_w[l], _row(gdn_a_log[l], 128), _row(gdn_dt_bias[l], 128),
                     _row(gdn_norm_w[l]), batch=batch, seq=seq, **TILES["gdn"])
        y_lru = _lru(proj, lru_conv_w[l], _row(lru_conv_b[l]), lru_wa[l].astype(BF16), _row(lru_ba[l]),
                     lru_wx[l].astype(BF16), _row(lru_bx[l]), _row(lru_lambda[l]),
                     batch=batch, seq=seq, **TILES["lru"])
        x1 = _outproj(xf, y_pool, y_gdn, y_lru, w_out_b, l, **TILES["outproj"])
        act = _ffn_up(x1, _row(norm2_w[l]), w_up_b, ffn_conv_w, l, seq=seq, **TILES["ffn_up"])
        xf = _ffn_down(act, w_down_b, x1, fnw, l, final_norm=(l == DEPTH - 1), **TILES["ffn_down"])
    return xf.reshape(batch, seq, d)
```

```python
import functools

import jax
import jax.numpy as jnp
from jax import lax
from jax.experimental import pallas as pl
from jax.experimental.pallas import tpu as pltpu

F32 = jnp.float32
BF16 = jnp.bfloat16

D_MODEL = 2048
DEPTH = 2
POOL_WINDOWS = (2, 4, 8, 16)
POOL_GROUP_DIM = 128
POOL_WIDTH = 512
GDN_HEADS = 6
HEAD_DIM = 128
GDN_WIDTH = 768
GDN_CONV = 4
GDN_CHUNK = 64
LRU_BLOCKS = 6
LRU_WIDTH = 768
LRU_CONV = 4
LRU_C = 8.0
D_FF = 3 * D_MODEL
FFN_CONV = 3
EPS = 1e-6

COL_Q, COL_K, COL_V, COL_Z = 0, 768, 1536, 2304
COL_XR, COL_GR = 3072, 3840
COL_POOL = 4608
COL_AB = 5120
PROJ_COLS = 5376

SUBLANES = 8
HALO = SUBLANES
CHUNK_SHIFT = 7
CHUNK = 1 << CHUNK_SHIFT
NEG_BIG = -1e30

V7X_VMEM_LIMIT = 56 * 1024 * 1024


def _cparams(sem):
    return pltpu.CompilerParams(dimension_semantics=sem, vmem_limit_bytes=V7X_VMEM_LIMIT)


def _dot(a, b):
    return jnp.dot(a, b, preferred_element_type=F32)


def _dot_nt(a, b):
    return lax.dot_general(a, b, (((1,), (1,)), ((), ())), preferred_element_type=F32)


def _dot_tn(a, b):
    return lax.dot_general(a, b, (((0,), (0,)), ((), ())), preferred_element_type=F32)


def _sigmoid(x):
    return 0.5 + 0.5 * jnp.tanh(0.5 * x)


def _softplus(x):
    return jnp.maximum(x, 0.0) + jnp.log1p(jnp.exp(-jnp.abs(x)))


def _gelu(x):
    return jax.nn.gelu(x, approximate=True)


def _rmsnorm(x, w):
    return x * lax.rsqrt(jnp.mean(x * x, axis=-1, keepdims=True) + EPS) * w


def _conv_from_ext(ext_ref, col0, width, w, taps, rows):
    x = ext_ref[0:HALO + rows, col0:col0 + width]
    acc = x[HALO:] * w[taps - 1:taps, :]
    for lag in range(1, taps):
        shifted = pltpu.roll(x, lag, 0)[HALO:]
        acc = acc + shifted * w[taps - 1 - lag:taps - lag, :]
    return acc


def _inproj_kernel(x_ref, nw_ref, w_ref, o_ref, h_ref):
    @pl.when(pl.program_id(1) == 0)
    def _():
        h_ref[...] = _rmsnorm(x_ref[...], nw_ref[...]).astype(BF16)

    o_ref[...] = _dot_nt(h_ref[...], w_ref[...])


def _inproj(x, nw, w, *, tm, tn):
    t, d = x.shape
    n = w.shape[0]
    return pl.pallas_call(
        _inproj_kernel,
        grid=(t // tm, n // tn),
        in_specs=[
            pl.BlockSpec((tm, d), lambda i, j: (i, 0)),
            pl.BlockSpec((1, d), lambda i, j: (0, 0)),
            pl.BlockSpec((tn, d), lambda i, j: (j, 0)),
        ],
        out_specs=pl.BlockSpec((tm, tn), lambda i, j: (i, j)),
        out_shape=jax.ShapeDtypeStruct((t, n), F32),
        scratch_shapes=[pltpu.VMEM((tm, d), BF16)],
        compiler_params=_cparams(("arbitrary", "arbitrary")),
        name="inproj",
    )(x, nw, w)


def _pool_kernel(u_ref, w_ref, b_ref, sc_ref, o_ref, prev_ref):
    s = pl.program_id(1)
    tp = u_ref.shape[0]

    @pl.when(s == 0)
    def _():
        prev_ref[...] = jnp.zeros_like(prev_ref)

    r = lax.broadcasted_iota(jnp.int32, (tp, tp), 0)
    c = lax.broadcasted_iota(jnp.int32, (tp, tp), 1)
    lag = r - c
    rh = lax.broadcasted_iota(jnp.int32, (128, 128), 0)
    ch = lax.broadcasted_iota(jnp.int32, (128, 128), 1)
    lag_h = rh + 128 - ch
    pos = s * tp + lax.broadcasted_iota(jnp.int32, (tp, 1), 0)

    def split3(v):
        v1 = v.astype(BF16)
        r1 = v - v1.astype(F32)
        v2 = r1.astype(BF16)
        v3 = (r1 - v2.astype(F32)).astype(BF16)
        return v1, v2, v3

    for g, win in enumerate(POOL_WINDOWS):
        sl = slice(g * POOL_GROUP_DIM, (g + 1) * POOL_GROUP_DIM)
        u = u_ref[:, sl]
        up = prev_ref[:, sl]
        band = ((lag >= 0) & (lag < win)).astype(BF16)
        band_h = (lag_h < win).astype(BF16)
        u1, u2, u3 = split3(u)
        p1, p2, p3 = split3(up)
        wsum = _dot(band, u1) + _dot(band, u2) + _dot(band, u3)
        head = _dot(band_h, p1) + _dot(band_h, p2) + _dot(band_h, p3)
        wsum = jnp.concatenate([wsum[:128] + head, wsum[128:]], axis=0)
        cnt = jnp.minimum(pos + 1, win).astype(F32)
        d = wsum / cnt - u
        y = _dot(d.astype(BF16), w_ref[g]) + b_ref[:, sl]
        o_ref[:, sl] = (y * sc_ref[:, sl]).astype(o_ref.dtype)

    prev_ref[...] = u_ref[tp - 128:tp, :]


def _pool(proj, pw, pb, psc, *, batch, seq, tp):
    nblk = seq // tp
    return pl.pallas_call(
        _pool_kernel,
        grid=(batch, nblk),
        in_specs=[
            pl.BlockSpec((tp, POOL_WIDTH), lambda b, s: (b * nblk + s, COL_POOL // POOL_WIDTH)),
            pl.BlockSpec((4, 128, 128), lambda b, s: (0, 0, 0)),
            pl.BlockSpec((1, POOL_WIDTH), lambda b, s: (0, 0)),
            pl.BlockSpec((1, POOL_WIDTH), lambda b, s: (0, 0)),
        ],
        out_specs=pl.BlockSpec((tp, POOL_WIDTH), lambda b, s: (b * nblk + s, 0)),
        out_shape=jax.ShapeDtypeStruct((batch * seq, POOL_WIDTH), BF16),
        scratch_shapes=[pltpu.VMEM((128, POOL_WIDTH), F32)],
        compiler_params=_cparams(("arbitrary", "arbitrary")),
        name="pool",
    )(proj, pw, pb, psc)


def _lru_kernel(xr_ref, gr_ref, cw_ref, cb_ref, wa_ref, ba_ref, wx_ref, bx_ref, lam_ref,
                o_ref, ext_ref, hc_ref):
    s = pl.program_id(1)
    tl = xr_ref.shape[0]

    @pl.when(s == 0)
    def _():
        ext_ref[0:HALO, :] = jnp.zeros((HALO, LRU_WIDTH), F32)
        hc_ref[...] = jnp.zeros_like(hc_ref)

    @pl.when(s > 0)
    def _():
        ext_ref[0:HALO, :] = ext_ref[tl:tl + HALO, :]

    ext_ref[HALO:HALO + tl, :] = xr_ref[...]

    row = lax.broadcasted_iota(jnp.int32, (tl, 128), 0)
    first = (row == 0) & (s == 0)
    groups = tl // SUBLANES
    sub = lax.broadcasted_iota(jnp.int32, (groups, SUBLANES, 128), 1)
    nsp = -LRU_C * _softplus(-lam_ref[...])
    cw = cw_ref[...]

    for blk in range(LRU_BLOCKS):
        sl = slice(blk * 128, (blk + 1) * 128)
        xc = _conv_from_ext(ext_ref, blk * 128, 128, cw[:, sl], LRU_CONV, tl) + cb_ref[:, sl]
        xb = xc.astype(BF16)
        r = _sigmoid(_dot(xb, wa_ref[blk]) + ba_ref[:, sl])
        i = _sigmoid(_dot(xb, wx_ref[blk]) + bx_ref[:, sl])
        log_a = r * nsp[:, sl]
        a = jnp.exp(log_a)
        mult = jnp.sqrt(-jnp.tanh(log_a) * (a * a + 1.0))
        mult = jnp.where(first, 1.0, mult)
        bv = mult * i * xc
        a3 = a.reshape(groups, SUBLANES, 128)
        b3 = bv.reshape(groups, SUBLANES, 128)
        sh = 1
        while sh < SUBLANES:
            keep = sub >= sh
            a_sh = jnp.where(keep, pltpu.roll(a3, sh, 1), 1.0)
            b_sh = jnp.where(keep, pltpu.roll(b3, sh, 1), 0.0)
            b3 = a3 * b_sh + b3
            a3 = a3 * a_sh
            sh *= 2
        hprev = hc_ref[:, sl]
        hs = []
        for g in range(groups):
            hg = a3[g] * hprev + b3[g]
            hs.append(hg)
            hprev = hg[SUBLANES - 1:SUBLANES, :]
        hc_ref[:, sl] = hprev
        h = jnp.concatenate(hs, axis=0)
        o_ref[:, sl] = (h * _gelu(gr_ref[:, sl])).astype(o_ref.dtype)


def _lru(proj, cw, cb, wa, ba, wx, bx, lam, *, batch, seq, tl):
    nblk = seq // tl
    vec = pl.BlockSpec((1, LRU_WIDTH), lambda b, s: (0, 0))
    mat = pl.BlockSpec((LRU_BLOCKS, 128, 128), lambda b, s: (0, 0, 0))
    return pl.pallas_call(
        _lru_kernel,
        grid=(batch, nblk),
        in_specs=[
            pl.BlockSpec((tl, LRU_WIDTH), lambda b, s: (b * nblk + s, COL_XR // LRU_WIDTH)),
            pl.BlockSpec((tl, LRU_WIDTH), lambda b, s: (b * nblk + s, COL_GR // LRU_WIDTH)),
            pl.BlockSpec((LRU_CONV, LRU_WIDTH), lambda b, s: (0, 0)),
            vec, mat, vec, mat, vec, vec,
        ],
        out_specs=pl.BlockSpec((tl, LRU_WIDTH), lambda b, s: (b * nblk + s, 0)),
        out_shape=jax.ShapeDtypeStruct((batch * seq, LRU_WIDTH), BF16),
        scratch_shapes=[pltpu.VMEM((tl + HALO, LRU_WIDTH), F32), pltpu.VMEM((1, LRU_WIDTH), F32)],
        compiler_params=_cparams(("arbitrary", "arbitrary")),
        name="lru",
    )(proj, proj, cw, cb, wa, ba, wx, bx, lam)


def _unit_lower_inverses(a_list, eye, level_masks):
    m8 = level_masks[0]
    zero = jnp.zeros((), BF16)
    ab = [a.astype(BF16) for a in a_list]
    n1 = [jnp.where(m8, -a, zero) for a in ab]
    p = [eye + n.astype(F32) for n in n1]
    n2 = [_dot(n, n).astype(BF16) for n in n1]
    p = [pi + _dot(pi.astype(BF16), n) for pi, n in zip(p, n2)]
    n4 = [_dot(n, n).astype(BF16) for n in n2]
    p = [pi + _dot(pi.astype(BF16), n) for pi, n in zip(p, n4)]
    for lvl in range(len(level_masks)):
        inner = level_masks[lvl]
        sel = ~inner if lvl + 1 == len(level_masks) else (level_masks[lvl + 1] & ~inner)
        pb = [pi.astype(BF16) for pi in p]
        t = [_dot(pbi, jnp.where(sel, a, zero)).astype(BF16) for pbi, a in zip(pb, ab)]
        p = [pi - _dot(ti, pbi) for pi, ti, pbi in zip(p, t, pb)]
    return p


def _gdn_kernel(q_ref, k_ref, v_ref, z_ref, ab_ref, cw_ref, alog_ref, dtb_ref, nw_ref,
                o_ref, ext_ref, qs_ref, ks_ref, vs_ref, gc_ref, beta_ref, gt_ref, eg_ref, ekd_ref,
                attn_ref, qd_ref, kd_ref, rk_ref, rv_ref, w_ref, u_ref, os_ref, st_ref):
    s = pl.program_id(1)
    ts = q_ref.shape[0]

    @pl.when(s == 0)
    def _():
        ext_ref[0:HALO, :] = jnp.zeros((HALO, 3 * GDN_WIDTH), F32)
        st_ref[...] = jnp.zeros_like(st_ref)

    @pl.when(s > 0)
    def _():
        ext_ref[0:HALO, :] = ext_ref[ts:ts + HALO, :]

    ext_ref[HALO:HALO + ts, 0:GDN_WIDTH] = q_ref[...]
    ext_ref[HALO:HALO + ts, GDN_WIDTH:2 * GDN_WIDTH] = k_ref[...]
    ext_ref[HALO:HALO + ts, 2 * GDN_WIDTH:3 * GDN_WIDTH] = v_ref[...]

    cw = cw_ref[...]
    for h in range(GDN_HEADS):
        for ti, dst in enumerate((qs_ref, ks_ref, vs_ref)):
            c0 = ti * GDN_WIDTH + h * HEAD_DIM
            t = _conv_from_ext(ext_ref, c0, HEAD_DIM, cw[:, c0:c0 + HEAD_DIM], GDN_CONV, ts)
            t = t * _sigmoid(t)
            if ti < 2:
                t = t * lax.rsqrt(jnp.sum(t * t, axis=-1, keepdims=True) + EPS)
            if ti == 0:
                t = t * (HEAD_DIM ** -0.5)
            dst[:, h * HEAD_DIM:(h + 1) * HEAD_DIM] = t

    ab = ab_ref[...]
    graw = -jnp.exp(alog_ref[...]) * _softplus(ab + dtb_ref[...])
    beta_ref[...] = _sigmoid(ab)
    rr = lax.broadcasted_iota(jnp.int32, (ts, ts), 0)
    cc = lax.broadcasted_iota(jnp.int32, (ts, ts), 1)
    ltri = ((rr >= cc) & ((rr >> CHUNK_SHIFT) == (cc >> CHUNK_SHIFT))).astype(BF16)
    g1 = graw.astype(BF16)
    r1 = graw - g1.astype(F32)
    g2 = r1.astype(BF16)
    g3 = (r1 - g2.astype(F32)).astype(BF16)
    gc_ref[...] = _dot(ltri, g1) + _dot(ltri, g2) + _dot(ltri, g3)

    ri = lax.broadcasted_iota(jnp.int32, (CHUNK, CHUNK), 0)
    ci = lax.broadcasted_iota(jnp.int32, (CHUNK, CHUNK), 1)
    causal = ri >= ci
    strict = ri > ci
    eye = (ri == ci).astype(F32)
    level_masks = tuple((ri >> sh) == (ci >> sh) for sh in (3, 4, 5, 6))
    nchunk = ts // CHUNK
    probs = [(c, h) for c in range(nchunk) for h in range(GDN_HEADS)]

    def rows_of(c):
        return slice(c * CHUNK, (c + 1) * CHUNK)

    def lanes_of(h):
        return slice(h * HEAD_DIM, (h + 1) * HEAD_DIM)

    egl = []
    for c in range(nchunk):
        gcb = gc_ref[rows_of(c), :]
        gt_ref[rows_of(c), :] = gcb.T
        gl = gcb[CHUNK - 1:CHUNK, :]
        eg_ref[rows_of(c), :] = jnp.exp(gcb)
        ekd_ref[rows_of(c), :] = jnp.exp(gl - gcb)
        egl.append(jnp.exp(gl))

    a_list = []
    for c, h in probs:
        rows, sl = rows_of(c), lanes_of(h)
        qh = qs_ref[rows, sl]
        kh = ks_ref[rows, sl]
        vh = vs_ref[rows, sl]
        gcol = gc_ref[rows, :][:, h:h + 1]
        grow = gt_ref[c * CHUNK + h:c * CHUNK + h + 1, :]
        dec = jnp.exp(jnp.where(causal, gcol - grow, NEG_BIG))
        bcol = beta_ref[rows, :][:, GDN_HEADS + h:GDN_HEADS + h + 1]
        egc = eg_ref[rows, :][:, h:h + 1]
        ekc = ekd_ref[rows, :][:, h:h + 1]
        kb = kh.astype(BF16)
        a_list.append(jnp.where(strict, bcol * _dot_nt(kb, kb) * dec, 0.0))
        attn_ref[rows, sl] = (_dot_nt(qh.astype(BF16), kb) * dec).astype(BF16)
        qd_ref[rows, sl] = (qh * egc).astype(BF16)
        kd_ref[rows, sl] = (kh * ekc).astype(BF16)
        rk_ref[rows, sl] = (kh * (bcol * egc)).astype(BF16)
        rv_ref[rows, sl] = (vh * bcol).astype(BF16)

    minv = _unit_lower_inverses(a_list, eye, level_masks)
    for (c, h), mi in zip(probs, minv):
        rows, sl = rows_of(c), lanes_of(h)
        mb = mi.astype(BF16)
        w_ref[rows, sl] = _dot(mb, rk_ref[rows, sl]).astype(BF16)
        u_ref[rows, sl] = _dot(mb, rv_ref[rows, sl])

    for c in range(nchunk):
        rows = rows_of(c)
        st = [st_ref[h] for h in range(GDN_HEADS)]
        sb = [x.astype(BF16) for x in st]
        vn = [u_ref[rows, lanes_of(h)] - _dot(w_ref[rows, lanes_of(h)], sb[h]) for h in range(GDN_HEADS)]
        vnb = [x.astype(BF16) for x in vn]
        for h in range(GDN_HEADS):
            sl = lanes_of(h)
            st_ref[h] = st[h] * egl[c][:, h:h + 1] + _dot_tn(kd_ref[rows, sl], vnb[h])
            os_ref[rows, sl] = _dot(qd_ref[rows, sl], sb[h]) + _dot(attn_ref[rows, sl], vnb[h])

    nw = nw_ref[...]
    for h in range(GDN_HEADS):
        sl = slice(h * HEAD_DIM, (h + 1) * HEAD_DIM)
        o = os_ref[:, sl]
        zz = z_ref[:, sl]
        y = o * lax.rsqrt(jnp.mean(o * o, axis=-1, keepdims=True) + EPS) * nw * (zz * _sigmoid(zz))
        o_ref[:, sl] = y.astype(o_ref.dtype)


def _gdn(proj, cw, alog, dtb, nw, *, batch, seq, ts):
    nblk = seq // ts

    def col(c0, width):
        return pl.BlockSpec((ts, width), lambda b, s: (b * nblk + s, c0 // width))

    small = pl.BlockSpec((1, 128), lambda b, s: (0, 0))
    return pl.pallas_call(
        _gdn_kernel,
        grid=(batch, nblk),
        in_specs=[
            col(COL_Q, GDN_WIDTH), col(COL_K, GDN_WIDTH), col(COL_V, GDN_WIDTH), col(COL_Z, GDN_WIDTH),
            col(COL_AB, 128),
            pl.BlockSpec((GDN_CONV, 3 * GDN_WIDTH), lambda b, s: (0, 0)),
            small, small, small,
        ],
        out_specs=pl.BlockSpec((ts, GDN_WIDTH), lambda b, s: (b * nblk + s, 0)),
        out_shape=jax.ShapeDtypeStruct((batch * seq, GDN_WIDTH), BF16),
        scratch_shapes=(
            [pltpu.VMEM((ts + HALO, 3 * GDN_WIDTH), F32)]
            + [pltpu.VMEM((ts, GDN_WIDTH), F32)] * 3
            + [pltpu.VMEM((ts, 128), F32)] * 5
            + [pltpu.VMEM((ts, GDN_WIDTH), BF16)] * 6
            + [pltpu.VMEM((ts, GDN_WIDTH), F32)] * 2
            + [pltpu.VMEM((GDN_HEADS, HEAD_DIM, HEAD_DIM), F32)]
        ),
        compiler_params=_cparams(("arbitrary", "arbitrary")),
        name="gdn",
    )(proj, proj, proj, proj, proj, cw, alog, dtb, nw)


def _outproj_kernel(x_ref, yp_ref, yg_ref, yl_ref, w_ref, o_ref):
    mixed = jnp.concatenate([yp_ref[...], yg_ref[...], yl_ref[...]], axis=1)
    o_ref[...] = x_ref[...] + _dot(mixed, w_ref[...])


def _outproj(x, yp, yg, yl, w, layer, *, tm):
    t, d = x.shape

    def rows(width):
        return pl.BlockSpec((tm, width), lambda i: (i, 0))

    return pl.pallas_call(
        _outproj_kernel,
        grid=(t // tm,),
        in_specs=[rows(d), rows(POOL_WIDTH), rows(GDN_WIDTH), rows(LRU_WIDTH),
                  pl.BlockSpec((None, d, d), lambda i: (layer, 0, 0))],
        out_specs=rows(d),
        out_shape=jax.ShapeDtypeStruct((t, d), F32),
        compiler_params=_cparams(("arbitrary",)),
        name="outproj",
    )(x, yp, yg, yl, w)


def _ffn_up_kernel(x_ref, nw_ref, wg_ref, wv_ref, cw_ref, o_ref, h_ref, g_ref, carry_ref,
                   *, tiles_per_seq, rb):
    i = pl.program_id(0)
    f = pl.program_id(1)
    tm = x_ref.shape[0]

    @pl.when(f == 0)
    def _():
        h_ref[...] = _rmsnorm(x_ref[...], nw_ref[...]).astype(BF16)

    halo = carry_ref[f]
    seq_start = (i % tiles_per_seq) == 0
    g_ref[0:HALO, :] = jnp.where(seq_start, jnp.zeros_like(halo), halo)
    cw = cw_ref[...]
    wg = wg_ref[...]
    wv = wv_ref[...]
    for r in range(tm // rb):
        hb = h_ref[r * rb:(r + 1) * rb, :]
        g_ref[HALO + r * rb:HALO + (r + 1) * rb, :] = _dot(hb, wg)
        val = _dot(hb, wv)
        conv = _conv_from_ext(g_ref.at[r * rb:r * rb + HALO + rb, :], 0, g_ref.shape[1], cw, FFN_CONV, rb)
        o_ref[r * rb:(r + 1) * rb, :] = (_gelu(conv) * val).astype(o_ref.dtype)
    carry_ref[f] = g_ref[tm:tm + HALO, :]


def _ffn_up(x, nw, w_up, cw, layer, *, seq, tm, tf, rb):
    t, d = x.shape
    nf = D_FF // tf
    kern = functools.partial(_ffn_up_kernel, tiles_per_seq=seq // tm, rb=rb)
    return pl.pallas_call(
        kern,
        grid=(t // tm, nf),
        in_specs=[
            pl.BlockSpec((tm, d), lambda i, f: (i, 0)),
            pl.BlockSpec((1, d), lambda i, f: (0, 0)),
            pl.BlockSpec((None, d, tf), lambda i, f: (layer, 0, f)),
            pl.BlockSpec((None, d, tf), lambda i, f: (layer, 0, nf + f)),
            pl.BlockSpec((None, FFN_CONV, tf), lambda i, f: (layer, 0, f)),
        ],
        out_specs=pl.BlockSpec((tm, tf), lambda i, f: (i, f)),
        out_shape=jax.ShapeDtypeStruct((t, D_FF), BF16),
        scratch_shapes=[
            pltpu.VMEM((tm, d), BF16),
            pltpu.VMEM((tm + HALO, tf), F32),
            pltpu.VMEM((nf, HALO, tf), F32),
        ],
        compiler_params=_cparams(("arbitrary", "arbitrary")),
        name="ffn_up",
    )(x, nw, w_up, w_up, cw)


def _ffn_down_kernel(a_ref, w_ref, x_ref, fnw_ref, o_ref, *, nj, final_norm):
    j = pl.program_id(1)
    tn = w_ref.shape[1]
    y = x_ref[...] + _dot(a_ref[...], w_ref[...])
    for jj in range(nj):
        @pl.when(j == jj)
        def _():
            o_ref[:, jj * tn:(jj + 1) * tn] = y

    if final_norm:
        @pl.when(j == nj - 1)
        def _():
            o_ref[...] = _rmsnorm(o_ref[...], fnw_ref[...])


def _ffn_down(act, w_down, x, fnw, layer, *, tm, tn, final_norm):
    t, d = x.shape
    nj = d // tn
    kern = functools.partial(_ffn_down_kernel, nj=nj, final_norm=final_norm)
    return pl.pallas_call(
        kern,
        grid=(t // tm, nj),
        in_specs=[
            pl.BlockSpec((tm, D_FF), lambda i, j: (i, 0)),
            pl.BlockSpec((None, D_FF, tn), lambda i, j: (layer, 0, j)),
            pl.BlockSpec((tm, tn), lambda i, j: (i, j)),
            pl.BlockSpec((1, d), lambda i, j: (0, 0)),
        ],
        out_specs=pl.BlockSpec((tm, d), lambda i, j: (i, 0)),
        out_shape=jax.ShapeDtypeStruct((t, d), F32),
        compiler_params=_cparams(("arbitrary", "arbitrary")),
        name="ffn_down",
    )(act, w_down, x, fnw)


def _pad_cols(w, width):
    return jnp.pad(w, ((0, 0), (0, width - w.shape[1])))


def _reorder_w_in(w, layer):
    gdn0 = POOL_WIDTH
    ab0 = gdn0 + 4 * GDN_WIDTH
    lru0 = ab0 + 2 * GDN_HEADS
    wt = jnp.transpose(w, (2, 0, 1))[:, layer, :].astype(BF16)
    rows = [wt[gdn0:ab0], wt[lru0:lru0 + 2 * LRU_WIDTH], wt[:POOL_WIDTH], wt[ab0:lru0]]
    used = sum(r.shape[0] for r in rows)
    rows.append(jnp.zeros((PROJ_COLS - used, wt.shape[1]), BF16))
    return jnp.concatenate(rows, axis=0)


TILES = dict(
    inproj=dict(tm=1024, tn=768),
    pool=dict(tp=256),
    gdn=dict(ts=2 * CHUNK),
    lru=dict(tl=256),
    outproj=dict(tm=512),
    ffn_up=dict(tm=1024, tf=512, rb=512),
    ffn_down=dict(tm=512, tn=1024),
)


def _row(v, width=None):
    v = v.reshape(1, -1).astype(F32)
    return v if width is None else _pad_cols(v, width)


def kernel(x, norm1_w, w_in, pool_w, pool_b, pool_scale, gdn_conv_w, gdn_a_log, gdn_dt_bias, gdn_norm_w,
           lru_conv_w, lru_conv_b, lru_wa, lru_ba, lru_wx, lru_bx, lru_lambda, w_out, norm2_w, ffn_up,
           ffn_conv_w, ffn_down, final_norm_w):
    batch, seq, d = x.shape
    xf = x.reshape(batch * seq, d)
    w_out_b = w_out.astype(BF16)
    w_up_b = ffn_up.astype(BF16)
    w_down_b = ffn_down.astype(BF16)
    fnw = _row(final_norm_w)
    for l in range(DEPTH):
        proj = _inproj(xf, _row(norm1_w[l]), _reorder_w_in(w_in, l), **TILES["inproj"])
        y_pool = _pool(proj, pool_w[l].astype(BF16), _row(pool_b[l]), _row(pool_scale[l]),
                       batch=batch, seq=seq, **TILES["pool"])
        y_gdn = _gdn(proj, gdn_conv_w[l], _row(gdn_a_log[l], 128), _row(gdn_dt_bias[l], 128),
                     _row(gdn_norm_w[l]), batch=batch, seq=seq, **TILES["gdn"])
        y_lru = _lru(proj, lru_conv_w[l], _row(lru_conv_b[l]), lru_wa[l].astype(BF16), _row(lru_ba[l]),
                     lru_wx[l].astype(BF16), _row(lru_bx[l]), _row(lru_lambda[l]),
                     batch=batch, seq=seq, **TILES["lru"])
        x1 = _outproj(xf, y_pool, y_gdn, y_lru, w_out_b, l, **TILES["outproj"])
        act = _ffn_up(x1, _row(norm2_w[l]), w_up_b, ffn_conv_w, l, seq=seq, **TILES["ffn_up"])
        xf = _ffn_down(act, w_down_b, x1, fnw, l, final_norm=(l == DEPTH - 1), **TILES["ffn_down"])
    return xf.reshape(batch, seq, d)
```

```python
import functools

import jax
import jax.numpy as jnp
from jax import lax
from jax.experimental import pallas as pl
from jax.experimental.pallas import tpu as pltpu

F32 = jnp.float32
BF16 = jnp.bfloat16

D_MODEL = 2048
DEPTH = 2
POOL_WINDOWS = (2, 4, 8, 16)
POOL_GROUP_DIM = 128
POOL_WIDTH = 512
GDN_HEADS = 6
HEAD_DIM = 128
GDN_WIDTH = 768
GDN_CONV = 4
GDN_CHUNK = 64
LRU_BLOCKS = 6
LRU_WIDTH = 768
LRU_CONV = 4
LRU_C = 8.0
D_FF = 3 * D_MODEL
FFN_CONV = 3
EPS = 1e-6

LANES = 128
COL_POOL = 0
COL_Q, COL_K, COL_V, COL_Z = 512, 1280, 2048, 2816
COL_AB = 3584
COL_XR, COL_GR = 3596, 4364
PROJ_COLS = 5376
GDN_IN_WIDTH = COL_AB + LANES
LRU_IN_WIDTH = PROJ_COLS - COL_AB
assert COL_AB % LRU_IN_WIDTH == 0 and LRU_IN_WIDTH % LANES == 0 and GDN_IN_WIDTH % LANES == 0

SUBLANES = 8
HALO = SUBLANES
POOL_HALO = 2 * SUBLANES
assert POOL_HALO >= max(POOL_WINDOWS) - 1
CHUNK_SHIFT = 7
CHUNK = 1 << CHUNK_SHIFT
NEG_BIG = -1e30

V7X_VMEM_LIMIT = 56 * 1024 * 1024


def _cparams(sem):
    return pltpu.CompilerParams(dimension_semantics=sem, vmem_limit_bytes=V7X_VMEM_LIMIT)


def _dot(a, b):
    return jnp.dot(a, b, preferred_element_type=F32)


def _dot_nt(a, b):
    return lax.dot_general(a, b, (((1,), (1,)), ((), ())), preferred_element_type=F32)


def _dot_tn(a, b):
    return lax.dot_general(a, b, (((0,), (0,)), ((), ())), preferred_element_type=F32)


def _sigmoid(x):
    return 0.5 + 0.5 * jnp.tanh(0.5 * x)


def _softplus(x):
    return jnp.maximum(x, 0.0) + jnp.log1p(jnp.exp(-jnp.abs(x)))


def _gelu(x):
    return jax.nn.gelu(x, approximate=True)


def _rmsnorm(x, w):
    return x * lax.rsqrt(jnp.mean(x * x, axis=-1, keepdims=True) + EPS) * w


def _conv_from_ext(ext_ref, col0, width, w, taps, rows):
    x = ext_ref[0:HALO + rows, col0:col0 + width]
    acc = x[HALO:] * w[taps - 1:taps, :]
    for lag in range(1, taps):
        shifted = pltpu.roll(x, lag, 0)[HALO:]
        acc = acc + shifted * w[taps - 1 - lag:taps - lag, :]
    return acc


def _inproj_kernel(x_ref, nw_ref, w_ref, o_ref, h_ref):
    @pl.when(pl.program_id(1) == 0)
    def _():
        h_ref[...] = _rmsnorm(x_ref[...], nw_ref[...]).astype(BF16)

    o_ref[...] = _dot_nt(h_ref[...], w_ref[...])


def _inproj(x, nw, w, *, tm, tn):
    t, d = x.shape
    n = w.shape[0]
    return pl.pallas_call(
        _inproj_kernel,
        grid=(t // tm, n // tn),
        in_specs=[
            pl.BlockSpec((tm, d), lambda i, j: (i, 0)),
            pl.BlockSpec((1, d), lambda i, j: (0, 0)),
            pl.BlockSpec((tn, d), lambda i, j: (j, 0)),
        ],
        out_specs=pl.BlockSpec((tm, tn), lambda i, j: (i, j)),
        out_shape=jax.ShapeDtypeStruct((t, n), F32),
        scratch_shapes=[pltpu.VMEM((tm, d), BF16)],
        compiler_params=_cparams(("arbitrary", "arbitrary")),
        name="inproj",
    )(x, nw, w)


def _pool_kernel(u_ref, w_ref, b_ref, sc_ref, o_ref, ext_ref):
    s = pl.program_id(1)
    tp = u_ref.shape[0]

    @pl.when(s == 0)
    def _():
        ext_ref[0:POOL_HALO, :] = jnp.zeros((POOL_HALO, POOL_WIDTH), F32)

    @pl.when(s > 0)
    def _():
        ext_ref[0:POOL_HALO, :] = ext_ref[tp:tp + POOL_HALO, :]

    ext_ref[POOL_HALO:POOL_HALO + tp, :] = u_ref[...]
    pos = s * tp + lax.broadcasted_iota(jnp.int32, (tp, 1), 0)

    for g, win in enumerate(POOL_WINDOWS):
        sl = slice(g * POOL_GROUP_DIM, (g + 1) * POOL_GROUP_DIM)
        x = ext_ref[:, sl]
        acc = x
        span = 1
        while span < win:
            acc = acc + pltpu.roll(acc, span, 0)
            span *= 2
        wsum = acc[POOL_HALO:]
        u = x[POOL_HALO:]
        cnt = jnp.minimum(pos + 1, win).astype(F32)
        d = wsum / cnt - u
        y = _dot(d.astype(BF16), w_ref[g]) + b_ref[:, sl]
        o_ref[:, sl] = (y * sc_ref[:, sl]).astype(o_ref.dtype)


def _pool(proj, pw, pb, psc, *, batch, seq, tp):
    nblk = seq // tp
    return pl.pallas_call(
        _pool_kernel,
        grid=(batch, nblk),
        in_specs=[
            pl.BlockSpec((tp, POOL_WIDTH), lambda b, s: (b * nblk + s, COL_POOL // POOL_WIDTH)),
            pl.BlockSpec((4, 128, 128), lambda b, s: (0, 0, 0)),
            pl.BlockSpec((1, POOL_WIDTH), lambda b, s: (0, 0)),
            pl.BlockSpec((1, POOL_WIDTH), lambda b, s: (0, 0)),
        ],
        out_specs=pl.BlockSpec((tp, POOL_WIDTH), lambda b, s: (b * nblk + s, 0)),
        out_shape=jax.ShapeDtypeStruct((batch * seq, POOL_WIDTH), BF16),
        scratch_shapes=[pltpu.VMEM((tp + POOL_HALO, POOL_WIDTH), F32)],
        compiler_params=_cparams(("arbitrary", "arbitrary")),
        name="pool",
    )(proj, pw, pb, psc)


def _lru_kernel(in_ref, cw_ref, cb_ref, wa_ref, ba_ref, wx_ref, bx_ref, lam_ref,
                o_ref, ext_ref, gr_ref, hc_ref):
    s = pl.program_id(1)
    tl = in_ref.shape[0]

    @pl.when(s == 0)
    def _():
        ext_ref[0:HALO, :] = jnp.zeros((HALO, LRU_WIDTH), F32)
        hc_ref[...] = jnp.zeros_like(hc_ref)

    @pl.when(s > 0)
    def _():
        ext_ref[0:HALO, :] = ext_ref[tl:tl + HALO, :]

    aligned = pltpu.roll(in_ref[...], LRU_IN_WIDTH - (COL_XR - COL_AB), 1)
    ext_ref[HALO:HALO + tl, :] = aligned[:, 0:LRU_WIDTH]
    gr_ref[...] = aligned[:, LRU_WIDTH:2 * LRU_WIDTH]

    row = lax.broadcasted_iota(jnp.int32, (tl, 128), 0)
    first = (row == 0) & (s == 0)
    groups = tl // SUBLANES
    sub = lax.broadcasted_iota(jnp.int32, (groups, SUBLANES, 128), 1)
    nsp = -LRU_C * _softplus(-lam_ref[...])
    cw = cw_ref[...]

    for blk in range(LRU_BLOCKS):
        sl = slice(blk * 128, (blk + 1) * 128)
        xc = _conv_from_ext(ext_ref, blk * 128, 128, cw[:, sl], LRU_CONV, tl) + cb_ref[:, sl]
        xb = xc.astype(BF16)
        r = _sigmoid(_dot(xb, wa_ref[blk]) + ba_ref[:, sl])
        i = _sigmoid(_dot(xb, wx_ref[blk]) + bx_ref[:, sl])
        log_a = r * nsp[:, sl]
        a = jnp.exp(log_a)
        mult = jnp.sqrt(-jnp.tanh(log_a) * (a * a + 1.0))
        mult = jnp.where(first, 1.0, mult)
        bv = mult * i * xc
        a3 = a.reshape(groups, SUBLANES, 128)
        b3 = bv.reshape(groups, SUBLANES, 128)
        sh = 1
        while sh < SUBLANES:
            keep = sub >= sh
            a_sh = jnp.where(keep, pltpu.roll(a3, sh, 1), 1.0)
            b_sh = jnp.where(keep, pltpu.roll(b3, sh, 1), 0.0)
            b3 = a3 * b_sh + b3
            a3 = a3 * a_sh
            sh *= 2
        hprev = hc_ref[:, sl]
        hs = []
        for g in range(groups):
            hg = a3[g] * hprev + b3[g]
            hs.append(hg)
            hprev = hg[SUBLANES - 1:SUBLANES, :]
        hc_ref[:, sl] = hprev
        h = jnp.concatenate(hs, axis=0)
        o_ref[:, sl] = (h * _gelu(gr_ref[:, sl])).astype(o_ref.dtype)


def _lru(proj, cw, cb, wa, ba, wx, bx, lam, *, batch, seq, tl):
    nblk = seq // tl
    vec = pl.BlockSpec((1, LRU_WIDTH), lambda b, s: (0, 0))
    mat = pl.BlockSpec((LRU_BLOCKS, 128, 128), lambda b, s: (0, 0, 0))
    return pl.pallas_call(
        _lru_kernel,
        grid=(batch, nblk),
        in_specs=[
            pl.BlockSpec((tl, LRU_IN_WIDTH), lambda b, s: (b * nblk + s, COL_AB // LRU_IN_WIDTH)),
            pl.BlockSpec((LRU_CONV, LRU_WIDTH), lambda b, s: (0, 0)),
            vec, mat, vec, mat, vec, vec,
        ],
        out_specs=pl.BlockSpec((tl, LRU_WIDTH), lambda b, s: (b * nblk + s, 0)),
        out_shape=jax.ShapeDtypeStruct((batch * seq, LRU_WIDTH), BF16),
        scratch_shapes=[pltpu.VMEM((tl + HALO, LRU_WIDTH), F32), pltpu.VMEM((tl, LRU_WIDTH), F32),
                        pltpu.VMEM((1, LRU_WIDTH), F32)],
        compiler_params=_cparams(("arbitrary", "arbitrary")),
        name="lru",
    )(proj, cw, cb, wa, ba, wx, bx, lam)


def _unit_lower_inverses(a_list, eye, level_masks):
    m8 = level_masks[0]
    zero = jnp.zeros((), BF16)
    ab = [a.astype(BF16) for a in a_list]
    n1 = [jnp.where(m8, -a, zero) for a in ab]
    p = [eye + n.astype(F32) for n in n1]
    n2 = [_dot(n, n).astype(BF16) for n in n1]
    p = [pi + _dot(pi.astype(BF16), n) for pi, n in zip(p, n2)]
    n4 = [_dot(n, n).astype(BF16) for n in n2]
    p = [pi + _dot(pi.astype(BF16), n) for pi, n in zip(p, n4)]
    for lvl in range(len(level_masks)):
        inner = level_masks[lvl]
        sel = ~inner if lvl + 1 == len(level_masks) else (level_masks[lvl + 1] & ~inner)
        pb = [pi.astype(BF16) for pi in p]
        t = [_dot(pbi, jnp.where(sel, a, zero)).astype(BF16) for pbi, a in zip(pb, ab)]
        p = [pi - _dot(ti, pbi) for pi, ti, pbi in zip(p, t, pb)]
    return p


def _gdn_kernel(in_ref, cw_ref, alog_ref, dtb_ref, nw_ref,
                o_ref, ext_ref, qs_ref, ks_ref, vs_ref, gc_ref, beta_ref, gt_ref, eg_ref, ekd_ref,
                attn_ref, qd_ref, kd_ref, rk_ref, rv_ref, w_ref, u_ref, os_ref, st_ref):
    s = pl.program_id(1)
    ts = in_ref.shape[0]

    @pl.when(s == 0)
    def _():
        ext_ref[0:HALO, :] = jnp.zeros((HALO, 3 * GDN_WIDTH), F32)
        st_ref[...] = jnp.zeros_like(st_ref)

    @pl.when(s > 0)
    def _():
        ext_ref[0:HALO, :] = ext_ref[ts:ts + HALO, :]

    ext_ref[HALO:HALO + ts, :] = in_ref[:, COL_Q:COL_Z]

    cw = cw_ref[...]
    for h in range(GDN_HEADS):
        for ti, dst in enumerate((qs_ref, ks_ref, vs_ref)):
            c0 = ti * GDN_WIDTH + h * HEAD_DIM
            t = _conv_from_ext(ext_ref, c0, HEAD_DIM, cw[:, c0:c0 + HEAD_DIM], GDN_CONV, ts)
            t = t * _sigmoid(t)
            if ti < 2:
                t = t * lax.rsqrt(jnp.sum(t * t, axis=-1, keepdims=True) + EPS)
            if ti == 0:
                t = t * (HEAD_DIM ** -0.5)
            dst[:, h * HEAD_DIM:(h + 1) * HEAD_DIM] = t

    ab = in_ref[:, COL_AB:COL_AB + LANES]
    graw = -jnp.exp(alog_ref[...]) * _softplus(ab + dtb_ref[...])
    beta_ref[...] = _sigmoid(ab)
    rr = lax.broadcasted_iota(jnp.int32, (ts, ts), 0)
    cc = lax.broadcasted_iota(jnp.int32, (ts, ts), 1)
    ltri = ((rr >= cc) & ((rr >> CHUNK_SHIFT) == (cc >> CHUNK_SHIFT))).astype(BF16)
    g1 = graw.astype(BF16)
    r1 = graw - g1.astype(F32)
    g2 = r1.astype(BF16)
    g3 = (r1 - g2.astype(F32)).astype(BF16)
    gc_ref[...] = _dot(ltri, g1) + _dot(ltri, g2) + _dot(ltri, g3)

    ri = lax.broadcasted_iota(jnp.int32, (CHUNK, CHUNK), 0)
    ci = lax.broadcasted_iota(jnp.int32, (CHUNK, CHUNK), 1)
    causal = ri >= ci
    strict = ri > ci
    eye = (ri == ci).astype(F32)
    level_masks = tuple((ri >> sh) == (ci >> sh) for sh in (3, 4, 5, 6))
    nchunk = ts // CHUNK
    probs = [(c, h) for c in range(nchunk) for h in range(GDN_HEADS)]

    def rows_of(c):
        return slice(c * CHUNK, (c + 1) * CHUNK)

    def lanes_of(h):
        return slice(h * HEAD_DIM, (h + 1) * HEAD_DIM)

    egl = []
    for c in range(nchunk):
        gcb = gc_ref[rows_of(c), :]
        gt_ref[rows_of(c), :] = gcb.T
        gl = gcb[CHUNK - 1:CHUNK, :]
        eg_ref[rows_of(c), :] = jnp.exp(gcb)
        ekd_ref[rows_of(c), :] = jnp.exp(gl - gcb)
        egl.append(jnp.exp(gl))

    a_list = []
    for c, h in probs:
        rows, sl = rows_of(c), lanes_of(h)
        qh = qs_ref[rows, sl]
        kh = ks_ref[rows, sl]
        vh = vs_ref[rows, sl]
        gcol = gc_ref[rows, :][:, h:h + 1]
        grow = gt_ref[c * CHUNK + h:c * CHUNK + h + 1, :]
        dec = jnp.exp(jnp.where(causal, gcol - grow, NEG_BIG))
        bcol = beta_ref[rows, :][:, GDN_HEADS + h:GDN_HEADS + h + 1]
        egc = eg_ref[rows, :][:, h:h + 1]
        ekc = ekd_ref[rows, :][:, h:h + 1]
        kb = kh.astype(BF16)
        a_list.append(jnp.where(strict, bcol * _dot_nt(kb, kb) * dec, 0.0))
        attn_ref[rows, sl] = (_dot_nt(qh.astype(BF16), kb) * dec).astype(BF16)
        qd_ref[rows, sl] = (qh * egc).astype(BF16)
        kd_ref[rows, sl] = (kh * ekc).astype(BF16)
        rk_ref[rows, sl] = (kh * (bcol * egc)).astype(BF16)
        rv_ref[rows, sl] = (vh * bcol).astype(BF16)

    minv = _unit_lower_inverses(a_list, eye, level_masks)
    for (c, h), mi in zip(probs, minv):
        rows, sl = rows_of(c), lanes_of(h)
        mb = mi.astype(BF16)
        w_ref[rows, sl] = _dot(mb, rk_ref[rows, sl]).astype(BF16)
        u_ref[rows, sl] = _dot(mb, rv_ref[rows, sl])

    for c in range(nchunk):
        rows = rows_of(c)
        st = [st_ref[h] for h in range(GDN_HEADS)]
        sb = [x.astype(BF16) for x in st]
        vn = [u_ref[rows, lanes_of(h)] - _dot(w_ref[rows, lanes_of(h)], sb[h]) for h in range(GDN_HEADS)]
        vnb = [x.astype(BF16) for x in vn]
        for h in range(GDN_HEADS):
            sl = lanes_of(h)
            st_ref[h] = st[h] * egl[c][:, h:h + 1] + _dot_tn(kd_ref[rows, sl], vnb[h])
            os_ref[rows, sl] = _dot(qd_ref[rows, sl], sb[h]) + _dot(attn_ref[rows, sl], vnb[h])

    nw = nw_ref[...]
    for h in range(GDN_HEADS):
        sl = slice(h * HEAD_DIM, (h + 1) * HEAD_DIM)
        o = os_ref[:, sl]
        zz = in_ref[:, COL_Z + h * HEAD_DIM:COL_Z + (h + 1) * HEAD_DIM]
        y = o * lax.rsqrt(jnp.mean(o * o, axis=-1, keepdims=True) + EPS) * nw * (zz * _sigmoid(zz))
        o_ref[:, sl] = y.astype(o_ref.dtype)


def _gdn(proj, cw, alog, dtb, nw, *, batch, seq, ts):
    nblk = seq // ts
    small = pl.BlockSpec((1, 128), lambda b, s: (0, 0))
    return pl.pallas_call(
        _gdn_kernel,
        grid=(batch, nblk),
        in_specs=[
            pl.BlockSpec((ts, GDN_IN_WIDTH), lambda b, s: (b * nblk + s, 0)),
            pl.BlockSpec((GDN_CONV, 3 * GDN_WIDTH), lambda b, s: (0, 0)),
            small, small, small,
        ],
        out_specs=pl.BlockSpec((ts, GDN_WIDTH), lambda b, s: (b * nblk + s, 0)),
        out_shape=jax.ShapeDtypeStruct((batch * seq, GDN_WIDTH), BF16),
        scratch_shapes=(
            [pltpu.VMEM((ts + HALO, 3 * GDN_WIDTH), F32)]
            + [pltpu.VMEM((ts, GDN_WIDTH), F32)] * 3
            + [pltpu.VMEM((ts, 128), F32)] * 5
            + [pltpu.VMEM((ts, GDN_WIDTH), BF16)] * 6
            + [pltpu.VMEM((ts, GDN_WIDTH), F32)] * 2
            + [pltpu.VMEM((GDN_HEADS, HEAD_DIM, HEAD_DIM), F32)]
        ),
        compiler_params=_cparams(("arbitrary", "arbitrary")),
        name="gdn",
    )(proj, cw, alog, dtb, nw)


def _outproj_kernel(x_ref, yp_ref, yg_ref, yl_ref, w_ref, o_ref):
    mixed = jnp.concatenate([yp_ref[...], yg_ref[...], yl_ref[...]], axis=1)
    o_ref[...] = x_ref[...] + _dot(mixed, w_ref[...])


def _outproj(x, yp, yg, yl, w, layer, *, tm):
    t, d = x.shape

    def rows(width):
        return pl.BlockSpec((tm, width), lambda i: (i, 0))

    return pl.pallas_call(
        _outproj_kernel,
        grid=(t // tm,),
        in_specs=[rows(d), rows(POOL_WIDTH), rows(GDN_WIDTH), rows(LRU_WIDTH),
                  pl.BlockSpec((None, d, d), lambda i: (layer, 0, 0))],
        out_specs=rows(d),
        out_shape=jax.ShapeDtypeStruct((t, d), F32),
        compiler_params=_cparams(("arbitrary",)),
        name="outproj",
    )(x, yp, yg, yl, w)


def _ffn_up_kernel(x_ref, nw_ref, wg_ref, wv_ref, cw_ref, o_ref, h_ref, g_ref, carry_ref,
                   *, tiles_per_seq, rb):
    i = pl.program_id(0)
    f = pl.program_id(1)
    tm = x_ref.shape[0]

    @pl.when(f == 0)
    def _():
        h_ref[...] = _rmsnorm(x_ref[...], nw_ref[...]).astype(BF16)

    halo = carry_ref[f]
    seq_start = (i % tiles_per_seq) == 0
    g_ref[0:HALO, :] = jnp.where(seq_start, jnp.zeros_like(halo), halo)
    cw = cw_ref[...]
    wg = wg_ref[...]
    wv = wv_ref[...]
    for r in range(tm // rb):
        hb = h_ref[r * rb:(r + 1) * rb, :]
        g_ref[HALO + r * rb:HALO + (r + 1) * rb, :] = _dot(hb, wg)
        val = _dot(hb, wv)
        conv = _conv_from_ext(g_ref.at[r * rb:r * rb + HALO + rb, :], 0, g_ref.shape[1], cw, FFN_CONV, rb)
        o_ref[r * rb:(r + 1) * rb, :] = (_gelu(conv) * val).astype(o_ref.dtype)
    carry_ref[f] = g_ref[tm:tm + HALO, :]


def _ffn_up(x, nw, w_up, cw, layer, *, seq, tm, tf, rb):
    t, d = x.shape
    nf = D_FF // tf
    kern = functools.partial(_ffn_up_kernel, tiles_per_seq=seq // tm, rb=rb)
    return pl.pallas_call(
        kern,
        grid=(t // tm, nf),
        in_specs=[
            pl.BlockSpec((tm, d), lambda i, f: (i, 0)),
            pl.BlockSpec((1, d), lambda i, f: (0, 0)),
            pl.BlockSpec((None, d, tf), lambda i, f: (layer, 0, f)),
            pl.BlockSpec((None, d, tf), lambda i, f: (layer, 0, nf + f)),
            pl.BlockSpec((None, FFN_CONV, tf), lambda i, f: (layer, 0, f)),
        ],
        out_specs=pl.BlockSpec((tm, tf), lambda i, f: (i, f)),
        out_shape=jax.ShapeDtypeStruct((t, D_FF), BF16),
        scratch_shapes=[
            pltpu.VMEM((tm, d), BF16),
            pltpu.VMEM((tm + HALO, tf), F32),
            pltpu.VMEM((nf, HALO, tf), F32),
        ],
        compiler_params=_cparams(("arbitrary", "arbitrary")),
        name="ffn_up",
    )(x, nw, w_up, w_up, cw)


def _ffn_down_kernel(a_ref, w_ref, x_ref, fnw_ref, o_ref, *, nj, final_norm):
    j = pl.program_id(1)
    tn = w_ref.shape[1]
    half = tn // 2
    a = a_ref[...]
    halves = [(c, x_ref[:, c:c + half] + _dot(a, w_ref[:, c:c + half])) for c in (0, half)]
    if not final_norm:
        for c, y in halves:
            o_ref[:, c:c + half] = y
        return
    for jj in range(nj):
        @pl.when(j == jj)
        def _():
            for c, y in halves:
                o_ref[:, jj * tn + c:jj * tn + c + half] = y

    @pl.when(j == nj - 1)
    def _():
        o_ref[...] = _rmsnorm(o_ref[...], fnw_ref[...])


def _ffn_down(act, w_down, x, fnw, layer, *, tm, tn, final_norm):
    t, d = x.shape
    nj = d // tn
    kern = functools.partial(_ffn_down_kernel, nj=nj, final_norm=final_norm)
    return pl.pallas_call(
        kern,
        grid=(t // tm, nj),
        in_specs=[
            pl.BlockSpec((tm, D_FF), lambda i, j: (i, 0)),
            pl.BlockSpec((None, D_FF, tn), lambda i, j: (layer, 0, j)),
            pl.BlockSpec((tm, tn), lambda i, j: (i, j)),
            pl.BlockSpec((1, d), lambda i, j: (0, 0)),
        ],
        out_specs=(pl.BlockSpec((tm, d), lambda i, j: (i, 0)) if final_norm
                   else pl.BlockSpec((tm, tn), lambda i, j: (i, j))),
        out_shape=jax.ShapeDtypeStruct((t, d), F32),
        compiler_params=_cparams(("arbitrary", "arbitrary")),
        name="ffn_down",
    )(act, w_down, x, fnw)


def _pad_cols(w, width):
    return jnp.pad(w, ((0, 0), (0, width - w.shape[1])))


def _prep_w_in(w, layer):
    wt = jnp.transpose(w, (2, 0, 1))[:, layer, :].astype(BF16)
    return jnp.pad(wt, ((0, PROJ_COLS - wt.shape[0]), (0, 0)))


TILES = dict(
    inproj=dict(tm=1024, tn=768),
    pool=dict(tp=256),
    gdn=dict(ts=2 * CHUNK),
    lru=dict(tl=256),
    outproj=dict(tm=512),
    ffn_up=dict(tm=1024, tf=512, rb=512),
    ffn_down=dict(tm=512, tn=1024),
)


def _row(v, width=None):
    v = v.reshape(1, -1).astype(F32)
    return v if width is None else _pad_cols(v, width)


def kernel(x, norm1_w, w_in, pool_w, pool_b, pool_scale, gdn_conv_w, gdn_a_log, gdn_dt_bias, gdn_norm_w,
           lru_conv_w, lru_conv_b, lru_wa, lru_ba, lru_wx, lru_bx, lru_lambda, w_out, norm2_w, ffn_up,
           ffn_conv_w, ffn_down, final_norm_w):
    batch, seq, d = x.shape
    xf = x.reshape(batch * seq, d)
    w_out_b = w_out.astype(BF16)
    w_up_b = ffn_up.astype(BF16)
    w_down_b = ffn_down.astype(BF16)
    fnw = _row(final_norm_w)
    for l in range(DEPTH):
        proj = _inproj(xf, _row(norm1_w[l]), _prep_w_in(w_in, l), **TILES["inproj"])
        y_pool = _pool(proj, pool_w[l].astype(BF16), _row(pool_b[l]), _row(pool_scale[l]),
                       batch=batch, seq=seq, **TILES["pool"])
        y_gdn = _gdn(proj, gdn_conv_w[l], _row(gdn_a_log[l], 128), _row(gdn_dt_bias[l], 128),
                     _row(gdn_norm_w[l]), batch=batch, seq=seq, **TILES["gdn"])
        y_lru = _lru(proj, lru_conv_w[l], _row(lru_conv_b[l]), lru_wa[l].astype(BF16), _row(lru_ba[l]),
                     lru_wx[l].astype(BF16), _row(lru_bx[l]), _row(lru_lambda[l]),
                     batch=batch, seq=seq, **TILES["lru"])
        x1 = _outproj(xf, y_pool, y_gdn, y_lru, w_out_b, l, **TILES["outproj"])
        act = _ffn_up(x1, _row(norm2_w[l]), w_up_b, ffn_conv_w, l, seq=seq, **TILES["ffn_up"])
        xf = _ffn_down(act, w_down_b, x1, fnw, l, final_norm=(l == DEPTH - 1), **TILES["ffn_down"])
    return xf.reshape(batch, seq, d)
```

```python
import functools

import jax
import jax.numpy as jnp
from jax import lax
from jax.experimental import pallas as pl
from jax.experimental.pallas import tpu as pltpu

F32 = jnp.float32
BF16 = jnp.bfloat16

D_MODEL = 2048
DEPTH = 2
POOL_WINDOWS = (2, 4, 8, 16)
POOL_GROUP_DIM = 128
POOL_WIDTH = 512
GDN_HEADS = 6
HEAD_DIM = 128
GDN_WIDTH = 768
GDN_CONV = 4
GDN_CHUNK = 64
LRU_BLOCKS = 6
LRU_WIDTH = 768
LRU_CONV = 4
LRU_C = 8.0
D_FF = 3 * D_MODEL
FFN_CONV = 3
EPS = 1e-6

LANES = 128
COL_POOL = 0
COL_Q, COL_K, COL_V, COL_Z = 512, 1280, 2048, 2816
COL_AB = 3584
COL_XR, COL_GR = 3596, 4364
PROJ_COLS = 5376
GDN_IN_WIDTH = COL_AB + LANES
LRU_IN_WIDTH = PROJ_COLS - COL_AB
assert COL_AB % LRU_IN_WIDTH == 0 and LRU_IN_WIDTH % LANES == 0 and GDN_IN_WIDTH % LANES == 0

SUBLANES = 8
HALO = SUBLANES
POOL_HALO = 2 * SUBLANES
assert POOL_HALO >= max(POOL_WINDOWS) - 1
CHUNK_SHIFT = 7
CHUNK = 1 << CHUNK_SHIFT
NEG_BIG = -1e30

V7X_VMEM_LIMIT = 56 * 1024 * 1024


def _cparams(sem):
    return pltpu.CompilerParams(dimension_semantics=sem, vmem_limit_bytes=V7X_VMEM_LIMIT)


def _dot(a, b):
    return jnp.dot(a, b, preferred_element_type=F32)


def _dot_nt(a, b):
    return lax.dot_general(a, b, (((1,), (1,)), ((), ())), preferred_element_type=F32)


def _dot_tn(a, b):
    return lax.dot_general(a, b, (((0,), (0,)), ((), ())), preferred_element_type=F32)


def _sigmoid(x):
    return 0.5 + 0.5 * jnp.tanh(0.5 * x)


def _softplus(x):
    return jnp.maximum(x, 0.0) + jnp.log1p(jnp.exp(-jnp.abs(x)))


def _gelu(x):
    return jax.nn.gelu(x, approximate=True)


def _rmsnorm(x, w):
    return x * lax.rsqrt(jnp.mean(x * x, axis=-1, keepdims=True) + EPS) * w


def _conv_from_ext(ext_ref, col0, width, w, taps, rows):
    x = ext_ref[0:HALO + rows, col0:col0 + width]
    acc = x[HALO:] * w[taps - 1:taps, :]
    for lag in range(1, taps):
        shifted = pltpu.roll(x, lag, 0)[HALO:]
        acc = acc + shifted * w[taps - 1 - lag:taps - lag, :]
    return acc


def _inproj_kernel(x_ref, nw_ref, w_ref, o_ref, h_ref):
    @pl.when(pl.program_id(1) == 0)
    def _():
        h_ref[...] = _rmsnorm(x_ref[...], nw_ref[...]).astype(BF16)

    o_ref[...] = _dot_nt(h_ref[...], w_ref[...])


def _inproj(x, nw, w, *, tm, tn):
    t, d = x.shape
    n = w.shape[0]
    return pl.pallas_call(
        _inproj_kernel,
        grid=(t // tm, n // tn),
        in_specs=[
            pl.BlockSpec((tm, d), lambda i, j: (i, 0)),
            pl.BlockSpec((1, d), lambda i, j: (0, 0)),
            pl.BlockSpec((tn, d), lambda i, j: (j, 0)),
        ],
        out_specs=pl.BlockSpec((tm, tn), lambda i, j: (i, j)),
        out_shape=jax.ShapeDtypeStruct((t, n), F32),
        scratch_shapes=[pltpu.VMEM((tm, d), BF16)],
        compiler_params=_cparams(("arbitrary", "arbitrary")),
        name="inproj",
    )(x, nw, w)


def _pool_kernel(u_ref, w_ref, b_ref, sc_ref, o_ref, ext_ref):
    s = pl.program_id(1)
    tp = u_ref.shape[0]

    @pl.when(s == 0)
    def _():
        ext_ref[0:POOL_HALO, :] = jnp.zeros((POOL_HALO, POOL_WIDTH), F32)

    @pl.when(s > 0)
    def _():
        ext_ref[0:POOL_HALO, :] = ext_ref[tp:tp + POOL_HALO, :]

    ext_ref[POOL_HALO:POOL_HALO + tp, :] = u_ref[...]
    pos = s * tp + lax.broadcasted_iota(jnp.int32, (tp, 1), 0)

    for g, win in enumerate(POOL_WINDOWS):
        sl = slice(g * POOL_GROUP_DIM, (g + 1) * POOL_GROUP_DIM)
        x = ext_ref[:, sl]
        acc = x
        span = 1
        while span < win:
            acc = acc + pltpu.roll(acc, span, 0)
            span *= 2
        wsum = acc[POOL_HALO:]
        u = x[POOL_HALO:]
        cnt = jnp.minimum(pos + 1, win).astype(F32)
        d = wsum / cnt - u
        y = _dot(d.astype(BF16), w_ref[g]) + b_ref[:, sl]
        o_ref[:, sl] = (y * sc_ref[:, sl]).astype(o_ref.dtype)


def _pool(proj, pw, pb, psc, *, batch, seq, tp):
    nblk = seq // tp
    return pl.pallas_call(
        _pool_kernel,
        grid=(batch, nblk),
        in_specs=[
            pl.BlockSpec((tp, POOL_WIDTH), lambda b, s: (b * nblk + s, COL_POOL // POOL_WIDTH)),
            pl.BlockSpec((4, 128, 128), lambda b, s: (0, 0, 0)),
            pl.BlockSpec((1, POOL_WIDTH), lambda b, s: (0, 0)),
            pl.BlockSpec((1, POOL_WIDTH), lambda b, s: (0, 0)),
        ],
        out_specs=pl.BlockSpec((tp, POOL_WIDTH), lambda b, s: (b * nblk + s, 0)),
        out_shape=jax.ShapeDtypeStruct((batch * seq, POOL_WIDTH), BF16),
        scratch_shapes=[pltpu.VMEM((tp + POOL_HALO, POOL_WIDTH), F32)],
        compiler_params=_cparams(("arbitrary", "arbitrary")),
        name="pool",
    )(proj, pw, pb, psc)


def _lru_kernel(in_ref, cw_ref, cb_ref, wa_ref, ba_ref, wx_ref, bx_ref, lam_ref,
                o_ref, ext_ref, gr_ref, hc_ref):
    s = pl.program_id(1)
    tl = in_ref.shape[0]

    @pl.when(s == 0)
    def _():
        ext_ref[0:HALO, :] = jnp.zeros((HALO, LRU_WIDTH), F32)
        hc_ref[...] = jnp.zeros_like(hc_ref)

    @pl.when(s > 0)
    def _():
        ext_ref[0:HALO, :] = ext_ref[tl:tl + HALO, :]

    aligned = pltpu.roll(in_ref[...], LRU_IN_WIDTH - (COL_XR - COL_AB), 1)
    ext_ref[HALO:HALO + tl, :] = aligned[:, 0:LRU_WIDTH]
    gr_ref[...] = aligned[:, LRU_WIDTH:2 * LRU_WIDTH]

    row = lax.broadcasted_iota(jnp.int32, (tl, 128), 0)
    first = (row == 0) & (s == 0)
    groups = tl // SUBLANES
    sub = lax.broadcasted_iota(jnp.int32, (groups, SUBLANES, 128), 1)
    nsp = -LRU_C * _softplus(-lam_ref[...])
    cw = cw_ref[...]

    for blk in range(LRU_BLOCKS):
        sl = slice(blk * 128, (blk + 1) * 128)
        xc = _conv_from_ext(ext_ref, blk * 128, 128, cw[:, sl], LRU_CONV, tl) + cb_ref[:, sl]
        xb = xc.astype(BF16)
        r = _sigmoid(_dot(xb, wa_ref[blk]) + ba_ref[:, sl])
        i = _sigmoid(_dot(xb, wx_ref[blk]) + bx_ref[:, sl])
        log_a = r * nsp[:, sl]
        a = jnp.exp(log_a)
        mult = jnp.sqrt(-jnp.tanh(log_a) * (a * a + 1.0))
        mult = jnp.where(first, 1.0, mult)
        bv = mult * i * xc
        a3 = a.reshape(groups, SUBLANES, 128)
        b3 = bv.reshape(groups, SUBLANES, 128)
        sh = 1
        while sh < SUBLANES:
            keep = sub >= sh
            a_sh = jnp.where(keep, pltpu.roll(a3, sh, 1), 1.0)
            b_sh = jnp.where(keep, pltpu.roll(b3, sh, 1), 0.0)
            b3 = a3 * b_sh + b3
            a3 = a3 * a_sh
            sh *= 2
        hprev = hc_ref[:, sl]
        hs = []
        for g in range(groups):
            hg = a3[g] * hprev + b3[g]
            hs.append(hg)
            hprev = hg[SUBLANES - 1:SUBLANES, :]
        hc_ref[:, sl] = hprev
        h = jnp.concatenate(hs, axis=0)
        o_ref[:, sl] = (h * _gelu(gr_ref[:, sl])).astype(o_ref.dtype)


def _lru(proj, cw, cb, wa, ba, wx, bx, lam, *, batch, seq, tl):
    nblk = seq // tl
    vec = pl.BlockSpec((1, LRU_WIDTH), lambda b, s: (0, 0))
    mat = pl.BlockSpec((LRU_BLOCKS, 128, 128), lambda b, s: (0, 0, 0))
    return pl.pallas_call(
        _lru_kernel,
        grid=(batch, nblk),
        in_specs=[
            pl.BlockSpec((tl, LRU_IN_WIDTH), lambda b, s: (b * nblk + s, COL_AB // LRU_IN_WIDTH)),
            pl.BlockSpec((LRU_CONV, LRU_WIDTH), lambda b, s: (0, 0)),
            vec, mat, vec, mat, vec, vec,
        ],
        out_specs=pl.BlockSpec((tl, LRU_WIDTH), lambda b, s: (b * nblk + s, 0)),
        out_shape=jax.ShapeDtypeStruct((batch * seq, LRU_WIDTH), BF16),
        scratch_shapes=[pltpu.VMEM((tl + HALO, LRU_WIDTH), F32), pltpu.VMEM((tl, LRU_WIDTH), F32),
                        pltpu.VMEM((1, LRU_WIDTH), F32)],
        compiler_params=_cparams(("arbitrary", "arbitrary")),
        name="lru",
    )(proj, cw, cb, wa, ba, wx, bx, lam)


def _unit_lower_inverses(a_list, eye, level_masks):
    m8 = level_masks[0]
    zero = jnp.zeros((), BF16)
    ab = [a.astype(BF16) for a in a_list]
    n1 = [jnp.where(m8, -a, zero) for a in ab]
    p = [eye + n.astype(F32) for n in n1]
    n2 = [_dot(n, n).astype(BF16) for n in n1]
    p = [pi + _dot(pi.astype(BF16), n) for pi, n in zip(p, n2)]
    n4 = [_dot(n, n).astype(BF16) for n in n2]
    p = [pi + _dot(pi.astype(BF16), n) for pi, n in zip(p, n4)]
    for lvl in range(len(level_masks)):
        inner = level_masks[lvl]
        sel = ~inner if lvl + 1 == len(level_masks) else (level_masks[lvl + 1] & ~inner)
        pb = [pi.astype(BF16) for pi in p]
        t = [_dot(pbi, jnp.where(sel, a, zero)).astype(BF16) for pbi, a in zip(pb, ab)]
        p = [pi - _dot(ti, pbi) for pi, ti, pbi in zip(p, t, pb)]
    return p


def _gdn_kernel(in_ref, cw_ref, alog_ref, dtb_ref, nw_ref,
                o_ref, ext_ref, qs_ref, ks_ref, vs_ref, gc_ref, beta_ref, gt_ref, eg_ref, ekd_ref,
                attn_ref, qd_ref, kd_ref, rk_ref, rv_ref, w_ref, u_ref, os_ref, st_ref):
    s = pl.program_id(1)
    ts = in_ref.shape[0]

    @pl.when(s == 0)
    def _():
        ext_ref[0:HALO, :] = jnp.zeros((HALO, 3 * GDN_WIDTH), F32)
        st_ref[...] = jnp.zeros_like(st_ref)

    @pl.when(s > 0)
    def _():
        ext_ref[0:HALO, :] = ext_ref[ts:ts + HALO, :]

    ext_ref[HALO:HALO + ts, :] = in_ref[:, COL_Q:COL_Z]

    cw = cw_ref[...]
    for h in range(GDN_HEADS):
        for ti, dst in enumerate((qs_ref, ks_ref, vs_ref)):
            c0 = ti * GDN_WIDTH + h * HEAD_DIM
            t = _conv_from_ext(ext_ref, c0, HEAD_DIM, cw[:, c0:c0 + HEAD_DIM], GDN_CONV, ts)
            t = t * _sigmoid(t)
            if ti < 2:
                t = t * lax.rsqrt(jnp.sum(t * t, axis=-1, keepdims=True) + EPS)
            if ti == 0:
                t = t * (HEAD_DIM ** -0.5)
            dst[:, h * HEAD_DIM:(h + 1) * HEAD_DIM] = t

    ab = in_ref[:, COL_AB:COL_AB + LANES]
    graw = -jnp.exp(alog_ref[...]) * _softplus(ab + dtb_ref[...])
    beta_ref[...] = _sigmoid(ab)
    rr = lax.broadcasted_iota(jnp.int32, (ts, ts), 0)
    cc = lax.broadcasted_iota(jnp.int32, (ts, ts), 1)
    ltri = ((rr >= cc) & ((rr >> CHUNK_SHIFT) == (cc >> CHUNK_SHIFT))).astype(BF16)
    g1 = graw.astype(BF16)
    r1 = graw - g1.astype(F32)
    g2 = r1.astype(BF16)
    g3 = (r1 - g2.astype(F32)).astype(BF16)
    gc_ref[...] = _dot(ltri, g1) + _dot(ltri, g2) + _dot(ltri, g3)

    ri = lax.broadcasted_iota(jnp.int32, (CHUNK, CHUNK), 0)
    ci = lax.broadcasted_iota(jnp.int32, (CHUNK, CHUNK), 1)
    causal = ri >= ci
    strict = ri > ci
    eye = (ri == ci).astype(F32)
    level_masks = tuple((ri >> sh) == (ci >> sh) for sh in (3, 4, 5, 6))
    nchunk = ts // CHUNK
    probs = [(c, h) for c in range(nchunk) for h in range(GDN_HEADS)]

    def rows_of(c):
        return slice(c * CHUNK, (c + 1) * CHUNK)

    def lanes_of(h):
        return slice(h * HEAD_DIM, (h + 1) * HEAD_DIM)

    egl = []
    for c in range(nchunk):
        gcb = gc_ref[rows_of(c), :]
        gt_ref[rows_of(c), :] = gcb.T
        gl = gcb[CHUNK - 1:CHUNK, :]
        eg_ref[rows_of(c), :] = jnp.exp(gcb)
        ekd_ref[rows_of(c), :] = jnp.exp(gl - gcb)
        egl.append(jnp.exp(gl))

    a_list = []
    for c, h in probs:
        rows, sl = rows_of(c), lanes_of(h)
        qh = qs_ref[rows, sl]
        kh = ks_ref[rows, sl]
        vh = vs_ref[rows, sl]
        gcol = gc_ref[rows, :][:, h:h + 1]
        grow = gt_ref[c * CHUNK + h:c * CHUNK + h + 1, :]
        dec = jnp.exp(jnp.where(causal, gcol - grow, NEG_BIG))
        bcol = beta_ref[rows, :][:, GDN_HEADS + h:GDN_HEADS + h + 1]
        egc = eg_ref[rows, :][:, h:h + 1]
        ekc = ekd_ref[rows, :][:, h:h + 1]
        kb = kh.astype(BF16)
        a_list.append(jnp.where(strict, bcol * _dot_nt(kb, kb) * dec, 0.0))
        attn_ref[rows, sl] = (_dot_nt(qh.astype(BF16), kb) * dec).astype(BF16)
        qd_ref[rows, sl] = (qh * egc).astype(BF16)
        kd_ref[rows, sl] = (kh * ekc).astype(BF16)
        rk_ref[rows, sl] = (kh * (bcol * egc)).astype(BF16)
        rv_ref[rows, sl] = (vh * bcol).astype(BF16)

    minv = _unit_lower_inverses(a_list, eye, level_masks)
    for (c, h), mi in zip(probs, minv):
        rows, sl = rows_of(c), lanes_of(h)
        mb = mi.astype(BF16)
        w_ref[rows, sl] = _dot(mb, rk_ref[rows, sl]).astype(BF16)
        u_ref[rows, sl] = _dot(mb, rv_ref[rows, sl])

    for c in range(nchunk):
        rows = rows_of(c)
        st = [st_ref[h] for h in range(GDN_HEADS)]
        sb = [x.astype(BF16) for x in st]
        vn = [u_ref[rows, lanes_of(h)] - _dot(w_ref[rows, lanes_of(h)], sb[h]) for h in range(GDN_HEADS)]
        vnb = [x.astype(BF16) for x in vn]
        for h in range(GDN_HEADS):
            sl = lanes_of(h)
            st_ref[h] = st[h] * egl[c][:, h:h + 1] + _dot_tn(kd_ref[rows, sl], vnb[h])
            os_ref[rows, sl] = _dot(qd_ref[rows, sl], sb[h]) + _dot(attn_ref[rows, sl], vnb[h])

    nw = nw_ref[...]
    for h in range(GDN_HEADS):
        sl = slice(h * HEAD_DIM, (h + 1) * HEAD_DIM)
        o = os_ref[:, sl]
        zz = in_ref[:, COL_Z + h * HEAD_DIM:COL_Z + (h + 1) * HEAD_DIM]
        y = o * lax.rsqrt(jnp.mean(o * o, axis=-1, keepdims=True) + EPS) * nw * (zz * _sigmoid(zz))
        o_ref[:, sl] = y.astype(o_ref.dtype)


def _gdn(proj, cw, alog, dtb, nw, *, batch, seq, ts):
    nblk = seq // ts
    small = pl.BlockSpec((1, 128), lambda b, s: (0, 0))
    return pl.pallas_call(
        _gdn_kernel,
        grid=(batch, nblk),
        in_specs=[
            pl.BlockSpec((ts, GDN_IN_WIDTH), lambda b, s: (b * nblk + s, 0)),
            pl.BlockSpec((GDN_CONV, 3 * GDN_WIDTH), lambda b, s: (0, 0)),
            small, small, small,
        ],
        out_specs=pl.BlockSpec((ts, GDN_WIDTH), lambda b, s: (b * nblk + s, 0)),
        out_shape=jax.ShapeDtypeStruct((batch * seq, GDN_WIDTH), BF16),
        scratch_shapes=(
            [pltpu.VMEM((ts + HALO, 3 * GDN_WIDTH), F32)]
            + [pltpu.VMEM((ts, GDN_WIDTH), F32)] * 3
            + [pltpu.VMEM((ts, 128), F32)] * 5
            + [pltpu.VMEM((ts, GDN_WIDTH), BF16)] * 6
            + [pltpu.VMEM((ts, GDN_WIDTH), F32)] * 2
            + [pltpu.VMEM((GDN_HEADS, HEAD_DIM, HEAD_DIM), F32)]
        ),
        compiler_params=_cparams(("arbitrary", "arbitrary")),
        name="gdn",
    )(proj, cw, alog, dtb, nw)


def _outproj_kernel(x_ref, yp_ref, yg_ref, yl_ref, w_ref, o_ref):
    mixed = jnp.concatenate([yp_ref[...], yg_ref[...], yl_ref[...]], axis=1)
    o_ref[...] = x_ref[...] + _dot(mixed, w_ref[...])


def _outproj(x, yp, yg, yl, w, layer, *, tm):
    t, d = x.shape

    def rows(width):
        return pl.BlockSpec((tm, width), lambda i: (i, 0))

    return pl.pallas_call(
        _outproj_kernel,
        grid=(t // tm,),
        in_specs=[rows(d), rows(POOL_WIDTH), rows(GDN_WIDTH), rows(LRU_WIDTH),
                  pl.BlockSpec((None, d, d), lambda i: (layer, 0, 0))],
        out_specs=rows(d),
        out_shape=jax.ShapeDtypeStruct((t, d), F32),
        compiler_params=_cparams(("arbitrary",)),
        name="outproj",
    )(x, yp, yg, yl, w)


def _ffn_up_kernel(x_ref, nw_ref, wg_ref, wv_ref, cw_ref, o_ref, h_ref, g_ref, carry_ref,
                   *, tiles_per_seq, rb):
    i = pl.program_id(0)
    f = pl.program_id(1)
    tm = x_ref.shape[0]

    @pl.when(f == 0)
    def _():
        h_ref[...] = _rmsnorm(x_ref[...], nw_ref[...]).astype(BF16)

    halo = carry_ref[f]
    seq_start = (i % tiles_per_seq) == 0
    g_ref[0:HALO, :] = jnp.where(seq_start, jnp.zeros_like(halo), halo)
    cw = cw_ref[...]
    wg = wg_ref[...]
    wv = wv_ref[...]
    for r in range(tm // rb):
        hb = h_ref[r * rb:(r + 1) * rb, :]
        g_ref[HALO + r * rb:HALO + (r + 1) * rb, :] = _dot(hb, wg)
        val = _dot(hb, wv)
        conv = _conv_from_ext(g_ref.at[r * rb:r * rb + HALO + rb, :], 0, g_ref.shape[1], cw, FFN_CONV, rb)
        o_ref[r * rb:(r + 1) * rb, :] = (_gelu(conv) * val).astype(o_ref.dtype)
    carry_ref[f] = g_ref[tm:tm + HALO, :]


def _ffn_up(x, nw, w_up, cw, layer, *, seq, tm, tf, rb):
    t, d = x.shape
    nf = D_FF // tf
    kern = functools.partial(_ffn_up_kernel, tiles_per_seq=seq // tm, rb=rb)
    return pl.pallas_call(
        kern,
        grid=(t // tm, nf),
        in_specs=[
            pl.BlockSpec((tm, d), lambda i, f: (i, 0)),
            pl.BlockSpec((1, d), lambda i, f: (0, 0)),
            pl.BlockSpec((None, d, tf), lambda i, f: (layer, 0, f)),
            pl.BlockSpec((None, d, tf), lambda i, f: (layer, 0, nf + f)),
            pl.BlockSpec((None, FFN_CONV, tf), lambda i, f: (layer, 0, f)),
        ],
        out_specs=pl.BlockSpec((tm, tf), lambda i, f: (i, f)),
        out_shape=jax.ShapeDtypeStruct((t, D_FF), BF16),
        scratch_shapes=[
            pltpu.VMEM((tm, d), BF16),
            pltpu.VMEM((tm + HALO, tf), F32),
            pltpu.VMEM((nf, HALO, tf), F32),
        ],
        compiler_params=_cparams(("arbitrary", "arbitrary")),
        name="ffn_up",
    )(x, nw, w_up, w_up, cw)


def _ffn_down_kernel(a_ref, w_ref, x_ref, fnw_ref, o_ref, *, nj, final_norm):
    j = pl.program_id(1)
    tn = w_ref.shape[1]
    half = tn // 2
    a = a_ref[...]
    halves = [(c, x_ref[:, c:c + half] + _dot(a, w_ref[:, c:c + half])) for c in (0, half)]
    if not final_norm:
        for c, y in halves:
            o_ref[:, c:c + half] = y
        return
    for jj in range(nj):
        @pl.when(j == jj)
        def _():
            for c, y in halves:
                o_ref[:, jj * tn + c:jj * tn + c + half] = y

    @pl.when(j == nj - 1)
    def _():
        o_ref[...] = _rmsnorm(o_ref[...], fnw_ref[...])


def _ffn_down(act, w_down, x, fnw, layer, *, tm, tn, final_norm):
    t, d = x.shape
    nj = d // tn
    kern = functools.partial(_ffn_down_kernel, nj=nj, final_norm=final_norm)
    return pl.pallas_call(
        kern,
        grid=(t // tm, nj),
        in_specs=[
            pl.BlockSpec((tm, D_FF), lambda i, j: (i, 0)),
            pl.BlockSpec((None, D_FF, tn), lambda i, j: (layer, 0, j)),
            pl.BlockSpec((tm, tn), lambda i, j: (i, j)),
            pl.BlockSpec((1, d), lambda i, j: (0, 0)),
        ],
        out_specs=(pl.BlockSpec((tm, d), lambda i, j: (i, 0)) if final_norm
                   else pl.BlockSpec((tm, tn), lambda i, j: (i, j))),
        out_shape=jax.ShapeDtypeStruct((t, d), F32),
        compiler_params=_cparams(("arbitrary", "arbitrary")),
        name="ffn_down",
    )(act, w_down, x, fnw)


def _pad_cols(w, width):
    return jnp.pad(w, ((0, 0), (0, width - w.shape[1])))


def _prep_w_in(w, layer):
    wt = jnp.transpose(w, (2, 0, 1))[:, layer, :].astype(BF16)
    return jnp.pad(wt, ((0, PROJ_COLS - wt.shape[0]), (0, 0)))


TILES = dict(
    inproj=dict(tm=1024, tn=1792),
    pool=dict(tp=1024),
    gdn=dict(ts=2 * CHUNK),
    lru=dict(tl=256),
    outproj=dict(tm=512),
    ffn_up=dict(tm=1024, tf=512, rb=512),
    ffn_down=dict(tm=1024, tn=512),
    ffn_down_final=dict(tm=512, tn=1024),
)


def _row(v, width=None):
    v = v.reshape(1, -1).astype(F32)
    return v if width is None else _pad_cols(v, width)


def kernel(x, norm1_w, w_in, pool_w, pool_b, pool_scale, gdn_conv_w, gdn_a_log, gdn_dt_bias, gdn_norm_w,
           lru_conv_w, lru_conv_b, lru_wa, lru_ba, lru_wx, lru_bx, lru_lambda, w_out, norm2_w, ffn_up,
           ffn_conv_w, ffn_down, final_norm_w):
    batch, seq, d = x.shape
    xf = x.reshape(batch * seq, d)
    w_out_b = w_out.astype(BF16)
    w_up_b = ffn_up.astype(BF16)
    w_down_b = ffn_down.astype(BF16)
    fnw = _row(final_norm_w)
    for l in range(DEPTH):
        proj = _inproj(xf, _row(norm1_w[l]), _prep_w_in(w_in, l), **TILES["inproj"])
        y_pool = _pool(proj, pool_w[l].astype(BF16), _row(pool_b[l]), _row(pool_scale[l]),
                       batch=batch, seq=seq, **TILES["pool"])
        y_gdn = _gdn(proj, gdn_conv_w[l], _row(gdn_a_log[l], 128), _row(gdn_dt_bias[l], 128),
                     _row(gdn_norm_w[l]), batch=batch, seq=seq, **TILES["gdn"])
        y_lru = _lru(proj, lru_conv_w[l], _row(lru_conv_b[l]), lru_wa[l].astype(BF16), _row(lru_ba[l]),
                     lru_wx[l].astype(BF16), _row(lru_bx[l]), _row(lru_lambda[l]),
                     batch=batch, seq=seq, **TILES["lru"])
        x1 = _outproj(xf, y_pool, y_gdn, y_lru, w_out_b, l, **TILES["outproj"])
        act = _ffn_up(x1, _row(norm2_w[l]), w_up_b, ffn_conv_w, l, seq=seq, **TILES["ffn_up"])
        last = l == DEPTH - 1
        xf = _ffn_down(act, w_down_b, x1, fnw, l, final_norm=last,
                       **TILES["ffn_down_final" if last else "ffn_down"])
    return xf.reshape(batch, seq, d)
```

```python
import functools

import jax
import jax.numpy as jnp
from jax import lax
from jax.experimental import pallas as pl
from jax.experimental.pallas import tpu as pltpu

F32 = jnp.float32
BF16 = jnp.bfloat16

D_MODEL = 2048
DEPTH = 2
POOL_WINDOWS = (2, 4, 8, 16)
POOL_GROUP_DIM = 128
POOL_WIDTH = 512
GDN_HEADS = 6
HEAD_DIM = 128
GDN_WIDTH = 768
GDN_CONV = 4
GDN_CHUNK = 64
LRU_BLOCKS = 6
LRU_WIDTH = 768
LRU_CONV = 4
LRU_C = 8.0
D_FF = 3 * D_MODEL
FFN_CONV = 3
EPS = 1e-6

LANES = 128
COL_POOL = 0
COL_Q, COL_K, COL_V, COL_Z = 512, 1280, 2048, 2816
COL_AB = 3584
COL_XR, COL_GR = 3596, 4364
PROJ_COLS = 5376
GDN_IN_WIDTH = COL_AB + LANES
LRU_IN_WIDTH = PROJ_COLS - COL_AB
assert COL_AB % LRU_IN_WIDTH == 0 and LRU_IN_WIDTH % LANES == 0 and GDN_IN_WIDTH % LANES == 0

SUBLANES = 8
HALO = SUBLANES
POOL_HALO = 2 * SUBLANES
assert POOL_HALO >= max(POOL_WINDOWS) - 1
CHUNK_SHIFT = 7
CHUNK = 1 << CHUNK_SHIFT
NEG_BIG = -1e30

V7X_VMEM_LIMIT = 56 * 1024 * 1024


def _cparams(sem):
    return pltpu.CompilerParams(dimension_semantics=sem, vmem_limit_bytes=V7X_VMEM_LIMIT)


def _dot(a, b):
    return jnp.dot(a, b, preferred_element_type=F32)


def _dot_nt(a, b):
    return lax.dot_general(a, b, (((1,), (1,)), ((), ())), preferred_element_type=F32)


def _dot_tn(a, b):
    return lax.dot_general(a, b, (((0,), (0,)), ((), ())), preferred_element_type=F32)


def _sigmoid(x):
    return 0.5 + 0.5 * jnp.tanh(0.5 * x)


def _silu(x):
    h = 0.5 * x
    return h + h * jnp.tanh(h)


def _softplus(x):
    return jnp.maximum(x, 0.0) + jnp.log1p(jnp.exp(-jnp.abs(x)))


def _gelu(x):
    return jax.nn.gelu(x, approximate=True)


def _rmsnorm(x, w):
    return x * lax.rsqrt(jnp.mean(x * x, axis=-1, keepdims=True) + EPS) * w


def _conv_from_ext(ext_ref, col0, width, w, taps, rows):
    x = ext_ref[0:HALO + rows, col0:col0 + width]
    acc = x[HALO:] * w[taps - 1:taps, :]
    for lag in range(1, taps):
        shifted = pltpu.roll(x, lag, 0)[HALO:]
        acc = acc + shifted * w[taps - 1 - lag:taps - lag, :]
    return acc


def _inproj_kernel(x_ref, nw_ref, w_ref, o_ref, h_ref):
    @pl.when(pl.program_id(1) == 0)
    def _():
        h_ref[...] = _rmsnorm(x_ref[...], nw_ref[...]).astype(BF16)

    o_ref[...] = _dot_nt(h_ref[...], w_ref[...])


def _inproj(x, nw, w, *, tm, tn):
    t, d = x.shape
    n = w.shape[0]
    return pl.pallas_call(
        _inproj_kernel,
        grid=(t // tm, n // tn),
        in_specs=[
            pl.BlockSpec((tm, d), lambda i, j: (i, 0)),
            pl.BlockSpec((1, d), lambda i, j: (0, 0)),
            pl.BlockSpec((tn, d), lambda i, j: (j, 0)),
        ],
        out_specs=pl.BlockSpec((tm, tn), lambda i, j: (i, j)),
        out_shape=jax.ShapeDtypeStruct((t, n), F32),
        scratch_shapes=[pltpu.VMEM((tm, d), BF16)],
        compiler_params=_cparams(("arbitrary", "arbitrary")),
        name="inproj",
    )(x, nw, w)


def _pool_kernel(u_ref, w_ref, b_ref, sc_ref, o_ref, ext_ref):
    s = pl.program_id(1)
    tp = u_ref.shape[0]

    @pl.when(s == 0)
    def _():
        ext_ref[0:POOL_HALO, :] = jnp.zeros((POOL_HALO, POOL_WIDTH), F32)

    @pl.when(s > 0)
    def _():
        ext_ref[0:POOL_HALO, :] = ext_ref[tp:tp + POOL_HALO, :]

    ext_ref[POOL_HALO:POOL_HALO + tp, :] = u_ref[...]
    pos = s * tp + lax.broadcasted_iota(jnp.int32, (tp, 1), 0)

    for g, win in enumerate(POOL_WINDOWS):
        sl = slice(g * POOL_GROUP_DIM, (g + 1) * POOL_GROUP_DIM)
        x = ext_ref[:, sl]
        acc = x
        span = 1
        while span < win:
            acc = acc + pltpu.roll(acc, span, 0)
            span *= 2
        wsum = acc[POOL_HALO:]
        u = x[POOL_HALO:]
        cnt = jnp.minimum(pos + 1, win).astype(F32)
        d = wsum / cnt - u
        y = _dot(d.astype(BF16), w_ref[g]) + b_ref[:, sl]
        o_ref[:, sl] = (y * sc_ref[:, sl]).astype(o_ref.dtype)


def _pool(proj, pw, pb, psc, *, batch, seq, tp):
    nblk = seq // tp
    return pl.pallas_call(
        _pool_kernel,
        grid=(batch, nblk),
        in_specs=[
            pl.BlockSpec((tp, POOL_WIDTH), lambda b, s: (b * nblk + s, COL_POOL // POOL_WIDTH)),
            pl.BlockSpec((4, 128, 128), lambda b, s: (0, 0, 0)),
            pl.BlockSpec((1, POOL_WIDTH), lambda b, s: (0, 0)),
            pl.BlockSpec((1, POOL_WIDTH), lambda b, s: (0, 0)),
        ],
        out_specs=pl.BlockSpec((tp, POOL_WIDTH), lambda b, s: (b * nblk + s, 0)),
        out_shape=jax.ShapeDtypeStruct((batch * seq, POOL_WIDTH), BF16),
        scratch_shapes=[pltpu.VMEM((tp + POOL_HALO, POOL_WIDTH), F32)],
        compiler_params=_cparams(("arbitrary", "arbitrary")),
        name="pool",
    )(proj, pw, pb, psc)


def _lru_kernel(in_ref, cw_ref, cb_ref, wa_ref, ba_ref, wx_ref, bx_ref, lam_ref,
                o_ref, ext_ref, gr_ref, hc_ref):
    s = pl.program_id(1)
    tl = in_ref.shape[0]

    @pl.when(s == 0)
    def _():
        ext_ref[0:HALO, :] = jnp.zeros((HALO, LRU_WIDTH), F32)
        hc_ref[...] = jnp.zeros_like(hc_ref)

    @pl.when(s > 0)
    def _():
        ext_ref[0:HALO, :] = ext_ref[tl:tl + HALO, :]

    aligned = pltpu.roll(in_ref[...], LRU_IN_WIDTH - (COL_XR - COL_AB), 1)
    ext_ref[HALO:HALO + tl, :] = aligned[:, 0:LRU_WIDTH]
    gr_ref[...] = aligned[:, LRU_WIDTH:2 * LRU_WIDTH]

    row = lax.broadcasted_iota(jnp.int32, (tl, 128), 0)
    first = (row == 0) & (s == 0)
    groups = tl // SUBLANES
    sub = lax.broadcasted_iota(jnp.int32, (groups, SUBLANES, 128), 1)
    nsp = -LRU_C * _softplus(-lam_ref[...])
    cw = cw_ref[...]

    for blk in range(LRU_BLOCKS):
        sl = slice(blk * 128, (blk + 1) * 128)
        xc = _conv_from_ext(ext_ref, blk * 128, 128, cw[:, sl], LRU_CONV, tl) + cb_ref[:, sl]
        xb = xc.astype(BF16)
        r = _sigmoid(_dot(xb, wa_ref[blk]) + ba_ref[:, sl])
        i = _sigmoid(_dot(xb, wx_ref[blk]) + bx_ref[:, sl])
        log_a = r * nsp[:, sl]
        a = jnp.exp(log_a)
        mult = jnp.sqrt(-jnp.tanh(log_a) * (a * a + 1.0))
        mult = jnp.where(first, 1.0, mult)
        bv = mult * i * xc
        a3 = a.reshape(groups, SUBLANES, 128)
        b3 = bv.reshape(groups, SUBLANES, 128)
        sh = 1
        while sh < SUBLANES:
            keep = sub >= sh
            a_sh = jnp.where(keep, pltpu.roll(a3, sh, 1), 1.0)
            b_sh = jnp.where(keep, pltpu.roll(b3, sh, 1), 0.0)
            b3 = a3 * b_sh + b3
            a3 = a3 * a_sh
            sh *= 2
        hprev = hc_ref[:, sl]
        hs = []
        for g in range(groups):
            hg = a3[g] * hprev + b3[g]
            hs.append(hg)
            hprev = hg[SUBLANES - 1:SUBLANES, :]
        hc_ref[:, sl] = hprev
        h = jnp.concatenate(hs, axis=0)
        o_ref[:, sl] = (h * _gelu(gr_ref[:, sl])).astype(o_ref.dtype)


def _lru(proj, cw, cb, wa, ba, wx, bx, lam, *, batch, seq, tl):
    nblk = seq // tl
    vec = pl.BlockSpec((1, LRU_WIDTH), lambda b, s: (0, 0))
    mat = pl.BlockSpec((LRU_BLOCKS, 128, 128), lambda b, s: (0, 0, 0))
    return pl.pallas_call(
        _lru_kernel,
        grid=(batch, nblk),
        in_specs=[
            pl.BlockSpec((tl, LRU_IN_WIDTH), lambda b, s: (b * nblk + s, COL_AB // LRU_IN_WIDTH)),
            pl.BlockSpec((LRU_CONV, LRU_WIDTH), lambda b, s: (0, 0)),
            vec, mat, vec, mat, vec, vec,
        ],
        out_specs=pl.BlockSpec((tl, LRU_WIDTH), lambda b, s: (b * nblk + s, 0)),
        out_shape=jax.ShapeDtypeStruct((batch * seq, LRU_WIDTH), BF16),
        scratch_shapes=[pltpu.VMEM((tl + HALO, LRU_WIDTH), F32), pltpu.VMEM((tl, LRU_WIDTH), F32),
                        pltpu.VMEM((1, LRU_WIDTH), F32)],
        compiler_params=_cparams(("arbitrary", "arbitrary")),
        name="lru",
    )(proj, cw, cb, wa, ba, wx, bx, lam)


def _unit_lower_inverses(a_list, eye, level_masks):
    m8 = level_masks[0]
    zero = jnp.zeros((), BF16)
    ab = [a.astype(BF16) for a in a_list]
    n1 = [jnp.where(m8, -a, zero) for a in ab]
    p = [eye + n.astype(F32) for n in n1]
    n2 = [_dot(n, n).astype(BF16) for n in n1]
    p = [pi + _dot(pi.astype(BF16), n) for pi, n in zip(p, n2)]
    n4 = [_dot(n, n).astype(BF16) for n in n2]
    p = [pi + _dot(pi.astype(BF16), n) for pi, n in zip(p, n4)]
    for lvl in range(len(level_masks)):
        inner = level_masks[lvl]
        sel = ~inner if lvl + 1 == len(level_masks) else (level_masks[lvl + 1] & ~inner)
        pb = [pi.astype(BF16) for pi in p]
        t = [_dot(pbi, jnp.where(sel, a, zero)).astype(BF16) for pbi, a in zip(pb, ab)]
        p = [pi - _dot(ti, pbi) for pi, ti, pbi in zip(p, t, pb)]
    return p


def _gdn_kernel(in_ref, cw_ref, alog_ref, dtb_ref, nw_ref,
                o_ref, ext_ref, qs_ref, ks_ref, vs_ref, gc_ref, beta_ref, gt_ref, eg_ref, ekd_ref,
                attn_ref, qd_ref, kd_ref, rk_ref, rv_ref, w_ref, u_ref, os_ref, st_ref):
    s = pl.program_id(1)
    ts = in_ref.shape[0]

    @pl.when(s == 0)
    def _():
        ext_ref[0:HALO, :] = jnp.zeros((HALO, 3 * GDN_WIDTH), F32)
        st_ref[...] = jnp.zeros_like(st_ref)

    @pl.when(s > 0)
    def _():
        ext_ref[0:HALO, :] = ext_ref[ts:ts + HALO, :]

    ext_ref[HALO:HALO + ts, :] = in_ref[:, COL_Q:COL_Z]

    cw = cw_ref[...]
    for h in range(GDN_HEADS):
        for ti, dst in enumerate((qs_ref, ks_ref, vs_ref)):
            c0 = ti * GDN_WIDTH + h * HEAD_DIM
            t = _conv_from_ext(ext_ref, c0, HEAD_DIM, cw[:, c0:c0 + HEAD_DIM], GDN_CONV, ts)
            t = _silu(t)
            if ti < 2:
                t = t * lax.rsqrt(jnp.sum(t * t, axis=-1, keepdims=True) + EPS)
            if ti == 0:
                t = t * (HEAD_DIM ** -0.5)
            dst[:, h * HEAD_DIM:(h + 1) * HEAD_DIM] = t

    ab = in_ref[:, COL_AB:COL_AB + LANES]
    graw = -jnp.exp(alog_ref[...]) * _softplus(ab + dtb_ref[...])
    beta_ref[...] = _sigmoid(ab)
    rr = lax.broadcasted_iota(jnp.int32, (ts, ts), 0)
    cc = lax.broadcasted_iota(jnp.int32, (ts, ts), 1)
    ltri = ((rr >= cc) & ((rr >> CHUNK_SHIFT) == (cc >> CHUNK_SHIFT))).astype(BF16)
    g1 = graw.astype(BF16)
    r1 = graw - g1.astype(F32)
    g2 = r1.astype(BF16)
    g3 = (r1 - g2.astype(F32)).astype(BF16)
    gc_ref[...] = _dot(ltri, g1) + _dot(ltri, g2) + _dot(ltri, g3)

    ri = lax.broadcasted_iota(jnp.int32, (CHUNK, CHUNK), 0)
    ci = lax.broadcasted_iota(jnp.int32, (CHUNK, CHUNK), 1)
    causal = ri >= ci
    strict = ri > ci
    eye = (ri == ci).astype(F32)
    level_masks = tuple((ri >> sh) == (ci >> sh) for sh in (3, 4, 5, 6))
    nchunk = ts // CHUNK
    probs = [(c, h) for c in range(nchunk) for h in range(GDN_HEADS)]

    def rows_of(c):
        return slice(c * CHUNK, (c + 1) * CHUNK)

    def lanes_of(h):
        return slice(h * HEAD_DIM, (h + 1) * HEAD_DIM)

    egl = []
    for c in range(nchunk):
        gcb = gc_ref[rows_of(c), :]
        gt_ref[rows_of(c), :] = gcb.T
        gl = gcb[CHUNK - 1:CHUNK, :]
        eg_ref[rows_of(c), :] = jnp.exp(gcb)
        ekd_ref[rows_of(c), :] = jnp.exp(gl - gcb)
        egl.append(jnp.exp(gl))

    a_list = []
    for c, h in probs:
        rows, sl = rows_of(c), lanes_of(h)
        qh = qs_ref[rows, sl]
        kh = ks_ref[rows, sl]
        vh = vs_ref[rows, sl]
        gcol = gc_ref[rows, :][:, h:h + 1]
        grow = gt_ref[c * CHUNK + h:c * CHUNK + h + 1, :]
        dec = jnp.exp(jnp.where(causal, gcol - grow, NEG_BIG))
        bcol = beta_ref[rows, :][:, GDN_HEADS + h:GDN_HEADS + h + 1]
        egc = eg_ref[rows, :][:, h:h + 1]
        ekc = ekd_ref[rows, :][:, h:h + 1]
        kb = kh.astype(BF16)
        a_list.append(jnp.where(strict, bcol * _dot_nt(kb, kb) * dec, 0.0))
        attn_ref[rows, sl] = (_dot_nt(qh.astype(BF16), kb) * dec).astype(BF16)
        qd_ref[rows, sl] = (qh * egc).astype(BF16)
        kd_ref[rows, sl] = (kh * ekc).astype(BF16)
        rk_ref[rows, sl] = (kh * (bcol * egc)).astype(BF16)
        rv_ref[rows, sl] = (vh * bcol).astype(BF16)

    minv = _unit_lower_inverses(a_list, eye, level_masks)
    for (c, h), mi in zip(probs, minv):
        rows, sl = rows_of(c), lanes_of(h)
        mb = mi.astype(BF16)
        w_ref[rows, sl] = _dot(mb, rk_ref[rows, sl]).astype(BF16)
        u_ref[rows, sl] = _dot(mb, rv_ref[rows, sl])

    for c in range(nchunk):
        rows = rows_of(c)
        st = [st_ref[h] for h in range(GDN_HEADS)]
        sb = [x.astype(BF16) for x in st]
        vn = [u_ref[rows, lanes_of(h)] - _dot(w_ref[rows, lanes_of(h)], sb[h]) for h in range(GDN_HEADS)]
        vnb = [x.astype(BF16) for x in vn]
        for h in range(GDN_HEADS):
            sl = lanes_of(h)
            st_ref[h] = st[h] * egl[c][:, h:h + 1] + _dot_tn(kd_ref[rows, sl], vnb[h])
            os_ref[rows, sl] = _dot(qd_ref[rows, sl], sb[h]) + _dot(attn_ref[rows, sl], vnb[h])

    nw = nw_ref[...]
    for h in range(GDN_HEADS):
        sl = slice(h * HEAD_DIM, (h + 1) * HEAD_DIM)
        o = os_ref[:, sl]
        zz = in_ref[:, COL_Z + h * HEAD_DIM:COL_Z + (h + 1) * HEAD_DIM]
        y = o * lax.rsqrt(jnp.mean(o * o, axis=-1, keepdims=True) + EPS) * nw * _silu(zz)
        o_ref[:, sl] = y.astype(o_ref.dtype)


def _gdn(proj, cw, alog, dtb, nw, *, batch, seq, ts):
    nblk = seq // ts
    small = pl.BlockSpec((1, 128), lambda b, s: (0, 0))
    return pl.pallas_call(
        _gdn_kernel,
        grid=(batch, nblk),
        in_specs=[
            pl.BlockSpec((ts, GDN_IN_WIDTH), lambda b, s: (b * nblk + s, 0)),
            pl.BlockSpec((GDN_CONV, 3 * GDN_WIDTH), lambda b, s: (0, 0)),
            small, small, small,
        ],
        out_specs=pl.BlockSpec((ts, GDN_WIDTH), lambda b, s: (b * nblk + s, 0)),
        out_shape=jax.ShapeDtypeStruct((batch * seq, GDN_WIDTH), BF16),
        scratch_shapes=(
            [pltpu.VMEM((ts + HALO, 3 * GDN_WIDTH), F32)]
            + [pltpu.VMEM((ts, GDN_WIDTH), F32)] * 3
            + [pltpu.VMEM((ts, 128), F32)] * 5
            + [pltpu.VMEM((ts, GDN_WIDTH), BF16)] * 6
            + [pltpu.VMEM((ts, GDN_WIDTH), F32)] * 2
            + [pltpu.VMEM((GDN_HEADS, HEAD_DIM, HEAD_DIM), F32)]
        ),
        compiler_params=_cparams(("arbitrary", "arbitrary")),
        name="gdn",
    )(proj, cw, alog, dtb, nw)


def _outproj_kernel(x_ref, yp_ref, yg_ref, yl_ref, w_ref, o_ref):
    mixed = jnp.concatenate([yp_ref[...], yg_ref[...], yl_ref[...]], axis=1)
    o_ref[...] = x_ref[...] + _dot(mixed, w_ref[...])


def _outproj(x, yp, yg, yl, w, layer, *, tm):
    t, d = x.shape

    def rows(width):
        return pl.BlockSpec((tm, width), lambda i: (i, 0))

    return pl.pallas_call(
        _outproj_kernel,
        grid=(t // tm,),
        in_specs=[rows(d), rows(POOL_WIDTH), rows(GDN_WIDTH), rows(LRU_WIDTH),
                  pl.BlockSpec((None, d, d), lambda i: (layer, 0, 0))],
        out_specs=rows(d),
        out_shape=jax.ShapeDtypeStruct((t, d), F32),
        compiler_params=_cparams(("arbitrary",)),
        name="outproj",
    )(x, yp, yg, yl, w)


def _ffn_up_kernel(x_ref, nw_ref, wg_ref, wv_ref, cw_ref, o_ref, h_ref, g_ref, carry_ref,
                   *, tiles_per_seq, rb):
    i = pl.program_id(0)
    f = pl.program_id(1)
    tm = x_ref.shape[0]

    @pl.when(f == 0)
    def _():
        h_ref[...] = _rmsnorm(x_ref[...], nw_ref[...]).astype(BF16)

    halo = carry_ref[f]
    seq_start = (i % tiles_per_seq) == 0
    g_ref[0:HALO, :] = jnp.where(seq_start, jnp.zeros_like(halo), halo)
    cw = cw_ref[...]
    wg = wg_ref[...].astype(BF16)
    wv = wv_ref[...].astype(BF16)
    for r in range(tm // rb):
        hb = h_ref[r * rb:(r + 1) * rb, :]
        g_ref[HALO + r * rb:HALO + (r + 1) * rb, :] = _dot(hb, wg)
        val = _dot(hb, wv)
        conv = _conv_from_ext(g_ref.at[r * rb:r * rb + HALO + rb, :], 0, g_ref.shape[1], cw, FFN_CONV, rb)
        o_ref[r * rb:(r + 1) * rb, :] = (_gelu(conv) * val).astype(o_ref.dtype)
    carry_ref[f] = g_ref[tm:tm + HALO, :]


def _ffn_up(x, nw, w_up, cw, layer, *, seq, tm, tf, rb):
    t, d = x.shape
    nf = D_FF // tf
    kern = functools.partial(_ffn_up_kernel, tiles_per_seq=seq // tm, rb=rb)
    return pl.pallas_call(
        kern,
        grid=(t // tm, nf),
        in_specs=[
            pl.BlockSpec((tm, d), lambda i, f: (i, 0)),
            pl.BlockSpec((1, d), lambda i, f: (0, 0)),
            pl.BlockSpec((None, d, tf), lambda i, f: (layer, 0, f)),
            pl.BlockSpec((None, d, tf), lambda i, f: (layer, 0, nf + f)),
            pl.BlockSpec((None, FFN_CONV, tf), lambda i, f: (layer, 0, f)),
        ],
        out_specs=pl.BlockSpec((tm, tf), lambda i, f: (i, f)),
        out_shape=jax.ShapeDtypeStruct((t, D_FF), BF16),
        scratch_shapes=[
            pltpu.VMEM((tm, d), BF16),
            pltpu.VMEM((tm + HALO, tf), F32),
            pltpu.VMEM((nf, HALO, tf), F32),
        ],
        compiler_params=_cparams(("arbitrary", "arbitrary")),
        name="ffn_up",
    )(x, nw, w_up, w_up, cw)


def _ffn_down_kernel(a_ref, w_ref, x_ref, fnw_ref, o_ref, *, nj, final_norm):
    j = pl.program_id(1)
    tn = w_ref.shape[1]
    half = tn // 2
    a = a_ref[...]
    halves = [(c, x_ref[:, c:c + half] + _dot(a, w_ref[:, c:c + half])) for c in (0, half)]
    if not final_norm:
        for c, y in halves:
            o_ref[:, c:c + half] = y
        return
    for jj in range(nj):
        @pl.when(j == jj)
        def _():
            for c, y in halves:
                o_ref[:, jj * tn + c:jj * tn + c + half] = y

    @pl.when(j == nj - 1)
    def _():
        o_ref[...] = _rmsnorm(o_ref[...], fnw_ref[...])


def _ffn_down(act, w_down, x, fnw, layer, *, tm, tn, final_norm):
    t, d = x.shape
    nj = d // tn
    kern = functools.partial(_ffn_down_kernel, nj=nj, final_norm=final_norm)
    return pl.pallas_call(
        kern,
        grid=(t // tm, nj),
        in_specs=[
            pl.BlockSpec((tm, D_FF), lambda i, j: (i, 0)),
            pl.BlockSpec((None, D_FF, tn), lambda i, j: (layer, 0, j)),
            pl.BlockSpec((tm, tn), lambda i, j: (i, j)),
            pl.BlockSpec((1, d), lambda i, j: (0, 0)),
        ],
        out_specs=(pl.BlockSpec((tm, d), lambda i, j: (i, 0)) if final_norm
                   else pl.BlockSpec((tm, tn), lambda i, j: (i, j))),
        out_shape=jax.ShapeDtypeStruct((t, d), F32),
        compiler_params=_cparams(("arbitrary", "arbitrary")),
        name="ffn_down",
    )(act, w_down, x, fnw)


def _pad_cols(w, width):
    return jnp.pad(w, ((0, 0), (0, width - w.shape[1])))


def _prep_w_in(w, layer):
    wt = jnp.transpose(w, (2, 0, 1))[:, layer, :].astype(BF16)
    return jnp.pad(wt, ((0, PROJ_COLS - wt.shape[0]), (0, 0)))


TILES = dict(
    inproj=dict(tm=1024, tn=1792),
    pool=dict(tp=1024),
    gdn=dict(ts=2 * CHUNK),
    lru=dict(tl=256),
    outproj=dict(tm=512),
    ffn_up=dict(tm=1024, tf=768, rb=512),
    ffn_down=dict(tm=1024, tn=512),
    ffn_down_final=dict(tm=512, tn=1024),
)


def _row(v, width=None):
    v = v.reshape(1, -1).astype(F32)
    return v if width is None else _pad_cols(v, width)


def kernel(x, norm1_w, w_in, pool_w, pool_b, pool_scale, gdn_conv_w, gdn_a_log, gdn_dt_bias, gdn_norm_w,
           lru_conv_w, lru_conv_b, lru_wa, lru_ba, lru_wx, lru_bx, lru_lambda, w_out, norm2_w, ffn_up,
           ffn_conv_w, ffn_down, final_norm_w):
    batch, seq, d = x.shape
    xf = x.reshape(batch * seq, d)
    w_out_b = w_out.astype(BF16)
    w_down_b = ffn_down.astype(BF16)
    fnw = _row(final_norm_w)
    for l in range(DEPTH):
        proj = _inproj(xf, _row(norm1_w[l]), _prep_w_in(w_in, l), **TILES["inproj"])
        y_pool = _pool(proj, pool_w[l].astype(BF16), _row(pool_b[l]), _row(pool_scale[l]),
                       batch=batch, seq=seq, **TILES["pool"])
        y_gdn = _gdn(proj, gdn_conv_w[l], _row(gdn_a_log[l], 128), _row(gdn_dt_bias[l], 128),
                     _row(gdn_norm_w[l]), batch=batch, seq=seq, **TILES["gdn"])
        y_lru = _lru(proj, lru_conv_w[l], _row(lru_conv_b[l]), lru_wa[l].astype(BF16), _row(lru_ba[l]),
                     lru_wx[l].astype(BF16), _row(lru_bx[l]), _row(lru_lambda[l]),
                     batch=batch, seq=seq, **TILES["lru"])
        x1 = _outproj(xf, y_pool, y_gdn, y_lru, w_out_b, l, **TILES["outproj"])
        act = _ffn_up(x1, _row(norm2_w[l]), ffn_up, ffn_conv_w, l, seq=seq, **TILES["ffn_up"])
        last = l == DEPTH - 1
        xf = _ffn_down(act, w_down_b, x1, fnw, l, final_norm=last,
                       **TILES["ffn_down_final" if last else "ffn_down"])
    return xf.reshape(batch, seq, d)
```

```python
import functools

import jax
import jax.numpy as jnp
from jax import lax
from jax.experimental import pallas as pl
from jax.experimental.pallas import tpu as pltpu

F32 = jnp.float32
BF16 = jnp.bfloat16

D_MODEL = 2048
DEPTH = 2
POOL_WINDOWS = (2, 4, 8, 16)
POOL_GROUP_DIM = 128
POOL_WIDTH = 512
GDN_HEADS = 6
HEAD_DIM = 128
GDN_WIDTH = 768
GDN_CONV = 4
GDN_CHUNK = 64
LRU_BLOCKS = 6
LRU_WIDTH = 768
LRU_CONV = 4
LRU_C = 8.0
D_FF = 3 * D_MODEL
FFN_CONV = 3
EPS = 1e-6

LANES = 128
COL_POOL = 0
COL_Q, COL_K, COL_V, COL_Z = 512, 1280, 2048, 2816
COL_AB = 3584
COL_XR, COL_GR = 3596, 4364
PROJ_COLS = 5376
GDN_IN_WIDTH = COL_AB + LANES
LRU_IN_WIDTH = PROJ_COLS - COL_AB
assert COL_AB % LRU_IN_WIDTH == 0 and LRU_IN_WIDTH % LANES == 0 and GDN_IN_WIDTH % LANES == 0

SUBLANES = 8
HALO = SUBLANES
POOL_HALO = 2 * SUBLANES
assert POOL_HALO >= max(POOL_WINDOWS) - 1
CHUNK_SHIFT = 7
CHUNK = 1 << CHUNK_SHIFT
NEG_BIG = -1e30
GDN_BATCH = 12

V7X_VMEM_BYTES = 64 * 1024 * 1024
V7X_VMEM_LIMIT = V7X_VMEM_BYTES - 4 * 1024 * 1024


def _cparams(sem):
    return pltpu.CompilerParams(dimension_semantics=sem, vmem_limit_bytes=V7X_VMEM_LIMIT)


def _dot(a, b):
    return jnp.dot(a, b, preferred_element_type=F32)


def _dot_nt(a, b):
    return lax.dot_general(a, b, (((1,), (1,)), ((), ())), preferred_element_type=F32)


def _dot_tn(a, b):
    return lax.dot_general(a, b, (((0,), (0,)), ((), ())), preferred_element_type=F32)


def _sigmoid(x):
    return 0.5 + 0.5 * jnp.tanh(0.5 * x)


def _silu(x):
    h = 0.5 * x
    return h + h * jnp.tanh(h)


def _softplus(x):
    return jnp.maximum(x, 0.0) + jnp.log1p(jnp.exp(-jnp.abs(x)))


def _gelu(x):
    return jax.nn.gelu(x, approximate=True)


def _rmsnorm(x, w):
    return x * lax.rsqrt(jnp.mean(x * x, axis=-1, keepdims=True) + EPS) * w


def _conv_from_ext(ext_ref, col0, width, w, taps, rows):
    x = ext_ref[0:HALO + rows, col0:col0 + width]
    acc = x[HALO:] * w[taps - 1:taps, :]
    for lag in range(1, taps):
        shifted = pltpu.roll(x, lag, 0)[HALO:]
        acc = acc + shifted * w[taps - 1 - lag:taps - lag, :]
    return acc


def _inproj_kernel(x_ref, nw_ref, w_ref, o_ref, h_ref, *, norm_rows):
    j = pl.program_id(1)
    tm = x_ref.shape[0]

    @pl.when(j == 0)
    def _():
        w = w_ref[...]
        for r0 in range(0, tm, norm_rows):
            rows = slice(r0, r0 + norm_rows)
            h = _rmsnorm(x_ref[rows, :], nw_ref[...]).astype(BF16)
            h_ref[rows, :] = h
            o_ref[rows, :] = _dot_nt(h, w)

    @pl.when(j > 0)
    def _():
        o_ref[...] = _dot_nt(h_ref[...], w_ref[...])


def _inproj(x, nw, w, *, tm, tn, norm_rows):
    t, d = x.shape
    n = w.shape[0]
    return pl.pallas_call(
        functools.partial(_inproj_kernel, norm_rows=norm_rows),
        grid=(t // tm, n // tn),
        in_specs=[
            pl.BlockSpec((tm, d), lambda i, j: (i, 0)),
            pl.BlockSpec((1, d), lambda i, j: (0, 0)),
            pl.BlockSpec((tn, d), lambda i, j: (j, 0)),
        ],
        out_specs=pl.BlockSpec((tm, tn), lambda i, j: (i, j)),
        out_shape=jax.ShapeDtypeStruct((t, n), F32),
        scratch_shapes=[pltpu.VMEM((tm, d), BF16)],
        compiler_params=_cparams(("arbitrary", "arbitrary")),
        name="inproj",
    )(x, nw, w)


def _pool_kernel(u_ref, w_ref, b_ref, sc_ref, o_ref, ext_ref):
    s = pl.program_id(1)
    tp = u_ref.shape[0]

    @pl.when(s == 0)
    def _():
        ext_ref[0:POOL_HALO, :] = jnp.zeros((POOL_HALO, POOL_WIDTH), F32)

    @pl.when(s > 0)
    def _():
        ext_ref[0:POOL_HALO, :] = ext_ref[tp:tp + POOL_HALO, :]

    ext_ref[POOL_HALO:POOL_HALO + tp, :] = u_ref[...]
    pos = s * tp + lax.broadcasted_iota(jnp.int32, (tp, 1), 0)

    for g, win in enumerate(POOL_WINDOWS):
        sl = slice(g * POOL_GROUP_DIM, (g + 1) * POOL_GROUP_DIM)
        x = ext_ref[:, sl]
        acc = x
        span = 1
        while span < win:
            acc = acc + pltpu.roll(acc, span, 0)
            span *= 2
        wsum = acc[POOL_HALO:]
        u = x[POOL_HALO:]
        cnt = jnp.minimum(pos + 1, win).astype(F32)
        d = wsum / cnt - u
        y = _dot(d.astype(BF16), w_ref[g]) + b_ref[:, sl]
        o_ref[:, sl] = (y * sc_ref[:, sl]).astype(o_ref.dtype)


def _pool(proj, pw, pb, psc, *, batch, seq, tp):
    nblk = seq // tp
    return pl.pallas_call(
        _pool_kernel,
        grid=(batch, nblk),
        in_specs=[
            pl.BlockSpec((tp, POOL_WIDTH), lambda b, s: (b * nblk + s, COL_POOL // POOL_WIDTH)),
            pl.BlockSpec((4, 128, 128), lambda b, s: (0, 0, 0)),
            pl.BlockSpec((1, POOL_WIDTH), lambda b, s: (0, 0)),
            pl.BlockSpec((1, POOL_WIDTH), lambda b, s: (0, 0)),
        ],
        out_specs=pl.BlockSpec((tp, POOL_WIDTH), lambda b, s: (b * nblk + s, 0)),
        out_shape=jax.ShapeDtypeStruct((batch * seq, POOL_WIDTH), BF16),
        scratch_shapes=[pltpu.VMEM((tp + POOL_HALO, POOL_WIDTH), F32)],
        compiler_params=_cparams(("arbitrary", "arbitrary")),
        name="pool",
    )(proj, pw, pb, psc)


def _lru_kernel(in_ref, cw_ref, cb_ref, wa_ref, ba_ref, wx_ref, bx_ref, lam_ref,
                o_ref, ext_ref, gr_ref, hc_ref):
    s = pl.program_id(1)
    tl = in_ref.shape[0]

    @pl.when(s == 0)
    def _():
        ext_ref[0:HALO, :] = jnp.zeros((HALO, LRU_WIDTH), F32)
        hc_ref[...] = jnp.zeros_like(hc_ref)

    @pl.when(s > 0)
    def _():
        ext_ref[0:HALO, :] = ext_ref[tl:tl + HALO, :]

    aligned = pltpu.roll(in_ref[...], LRU_IN_WIDTH - (COL_XR - COL_AB), 1)
    ext_ref[HALO:HALO + tl, :] = aligned[:, 0:LRU_WIDTH]
    gr_ref[...] = aligned[:, LRU_WIDTH:2 * LRU_WIDTH]

    row = lax.broadcasted_iota(jnp.int32, (tl, 128), 0)
    first = (row == 0) & (s == 0)
    groups = tl // SUBLANES
    sub = lax.broadcasted_iota(jnp.int32, (groups, SUBLANES, 128), 1)
    nsp = -LRU_C * _softplus(-lam_ref[...])
    cw = cw_ref[...]

    for blk in range(LRU_BLOCKS):
        sl = slice(blk * 128, (blk + 1) * 128)
        xc = _conv_from_ext(ext_ref, blk * 128, 128, cw[:, sl], LRU_CONV, tl) + cb_ref[:, sl]
        xb = xc.astype(BF16)
        r = _sigmoid(_dot(xb, wa_ref[blk]) + ba_ref[:, sl])
        i = _sigmoid(_dot(xb, wx_ref[blk]) + bx_ref[:, sl])
        log_a = r * nsp[:, sl]
        a = jnp.exp(log_a)
        mult = jnp.sqrt(-jnp.tanh(log_a) * (a * a + 1.0))
        mult = jnp.where(first, 1.0, mult)
        bv = mult * i * xc
        a3 = a.reshape(groups, SUBLANES, 128)
        b3 = bv.reshape(groups, SUBLANES, 128)
        sh = 1
        while sh < SUBLANES:
            keep = sub >= sh
            a_sh = jnp.where(keep, pltpu.roll(a3, sh, 1), 1.0)
            b_sh = jnp.where(keep, pltpu.roll(b3, sh, 1), 0.0)
            b3 = a3 * b_sh + b3
            a3 = a3 * a_sh
            sh *= 2
        hprev = hc_ref[:, sl]
        hs = []
        for g in range(groups):
            hg = a3[g] * hprev + b3[g]
            hs.append(hg)
            hprev = hg[SUBLANES - 1:SUBLANES, :]
        hc_ref[:, sl] = hprev
        h = jnp.concatenate(hs, axis=0)
        o_ref[:, sl] = (h * _gelu(gr_ref[:, sl])).astype(o_ref.dtype)


def _lru(proj, cw, cb, wa, ba, wx, bx, lam, *, batch, seq, tl):
    nblk = seq // tl
    vec = pl.BlockSpec((1, LRU_WIDTH), lambda b, s: (0, 0))
    mat = pl.BlockSpec((LRU_BLOCKS, 128, 128), lambda b, s: (0, 0, 0))
    return pl.pallas_call(
        _lru_kernel,
        grid=(batch, nblk),
        in_specs=[
            pl.BlockSpec((tl, LRU_IN_WIDTH), lambda b, s: (b * nblk + s, COL_AB // LRU_IN_WIDTH)),
            pl.BlockSpec((LRU_CONV, LRU_WIDTH), lambda b, s: (0, 0)),
            vec, mat, vec, mat, vec, vec,
        ],
        out_specs=pl.BlockSpec((tl, LRU_WIDTH), lambda b, s: (b * nblk + s, 0)),
        out_shape=jax.ShapeDtypeStruct((batch * seq, LRU_WIDTH), BF16),
        scratch_shapes=[pltpu.VMEM((tl + HALO, LRU_WIDTH), F32), pltpu.VMEM((tl, LRU_WIDTH), F32),
                        pltpu.VMEM((1, LRU_WIDTH), F32)],
        compiler_params=_cparams(("arbitrary", "arbitrary")),
        name="lru",
    )(proj, cw, cb, wa, ba, wx, bx, lam)


def _unit_lower_inverses(a_list, eye, level_masks):
    m8 = level_masks[0]
    zero = jnp.zeros((), BF16)
    ab = [a.astype(BF16) for a in a_list]
    n1 = [jnp.where(m8, -a, zero) for a in ab]
    p = [eye + n.astype(F32) for n in n1]
    n2 = [_dot(n, n).astype(BF16) for n in n1]
    p = [pi + _dot(pi.astype(BF16), n) for pi, n in zip(p, n2)]
    n4 = [_dot(n, n).astype(BF16) for n in n2]
    p = [pi + _dot(pi.astype(BF16), n) for pi, n in zip(p, n4)]
    for lvl in range(len(level_masks)):
        inner = level_masks[lvl]
        sel = ~inner if lvl + 1 == len(level_masks) else (level_masks[lvl + 1] & ~inner)
        pb = [pi.astype(BF16) for pi in p]
        t = [_dot(pbi, jnp.where(sel, a, zero)).astype(BF16) for pbi, a in zip(pb, ab)]
        p = [pi - _dot(ti, pbi) for pi, ti, pbi in zip(p, t, pb)]
    return p


def _gdn_kernel(in_ref, cw_ref, alog_ref, dtb_ref, nw_ref,
                o_ref, ext_ref, qs_ref, ks_ref, vs_ref, gc_ref, beta_ref, gt_ref, eg_ref, ekd_ref,
                attn_ref, qd_ref, kd_ref, rk_ref, rv_ref, w_ref, u_ref, os_ref, st_ref):
    s = pl.program_id(1)
    ts = in_ref.shape[0]

    @pl.when(s == 0)
    def _():
        ext_ref[0:HALO, :] = jnp.zeros((HALO, 3 * GDN_WIDTH), F32)
        st_ref[...] = jnp.zeros_like(st_ref)

    @pl.when(s > 0)
    def _():
        ext_ref[0:HALO, :] = ext_ref[ts:ts + HALO, :]

    ext_ref[HALO:HALO + ts, :] = in_ref[:, COL_Q:COL_Z]

    cw = cw_ref[...]
    for h in range(GDN_HEADS):
        for ti, dst in enumerate((qs_ref, ks_ref, vs_ref)):
            c0 = ti * GDN_WIDTH + h * HEAD_DIM
            t = _conv_from_ext(ext_ref, c0, HEAD_DIM, cw[:, c0:c0 + HEAD_DIM], GDN_CONV, ts)
            t = _silu(t)
            if ti < 2:
                t = t * lax.rsqrt(jnp.sum(t * t, axis=-1, keepdims=True) + EPS)
            if ti == 0:
                t = t * (HEAD_DIM ** -0.5)
            dst[:, h * HEAD_DIM:(h + 1) * HEAD_DIM] = t

    ab = in_ref[:, COL_AB:COL_AB + LANES]
    graw = -jnp.exp(alog_ref[...]) * _softplus(ab + dtb_ref[...])
    beta_ref[...] = _sigmoid(ab)
    rr = lax.broadcasted_iota(jnp.int32, (ts, ts), 0)
    cc = lax.broadcasted_iota(jnp.int32, (ts, ts), 1)
    ltri = ((rr >= cc) & ((rr >> CHUNK_SHIFT) == (cc >> CHUNK_SHIFT))).astype(BF16)
    g1 = graw.astype(BF16)
    r1 = graw - g1.astype(F32)
    g2 = r1.astype(BF16)
    g3 = (r1 - g2.astype(F32)).astype(BF16)
    gc_ref[...] = _dot(ltri, g1) + _dot(ltri, g2) + _dot(ltri, g3)

    ri = lax.broadcasted_iota(jnp.int32, (CHUNK, CHUNK), 0)
    ci = lax.broadcasted_iota(jnp.int32, (CHUNK, CHUNK), 1)
    causal = ri >= ci
    strict = ri > ci
    eye = (ri == ci).astype(F32)
    level_masks = tuple((ri >> sh) == (ci >> sh) for sh in (3, 4, 5, 6))
    nchunk = ts // CHUNK
    probs = [(c, h) for c in range(nchunk) for h in range(GDN_HEADS)]

    def rows_of(c):
        return slice(c * CHUNK, (c + 1) * CHUNK)

    def lanes_of(h):
        return slice(h * HEAD_DIM, (h + 1) * HEAD_DIM)

    def solve_batch(batch, a_batch):
        for (c, h), mi in zip(batch, _unit_lower_inverses(a_batch, eye, level_masks)):
            rows, sl = rows_of(c), lanes_of(h)
            mb = mi.astype(BF16)
            w_ref[rows, sl] = _dot(mb, rk_ref[rows, sl]).astype(BF16)
            u_ref[rows, sl] = _dot(mb, rv_ref[rows, sl])

    egl = []
    for c in range(nchunk):
        gcb = gc_ref[rows_of(c), :]
        gt_ref[rows_of(c), :] = gcb.T
        gl = gcb[CHUNK - 1:CHUNK, :]
        eg_ref[rows_of(c), :] = jnp.exp(gcb)
        ekd_ref[rows_of(c), :] = jnp.exp(gl - gcb)
        egl.append(jnp.exp(gl))

    def intra_chunk(c, h):
        rows, sl = rows_of(c), lanes_of(h)
        qh = qs_ref[rows, sl]
        kh = ks_ref[rows, sl]
        vh = vs_ref[rows, sl]
        gcol = gc_ref[rows, :][:, h:h + 1]
        grow = gt_ref[c * CHUNK + h:c * CHUNK + h + 1, :]
        dec = jnp.exp(jnp.where(causal, gcol - grow, NEG_BIG))
        bcol = beta_ref[rows, :][:, GDN_HEADS + h:GDN_HEADS + h + 1]
        egc = eg_ref[rows, :][:, h:h + 1]
        ekc = ekd_ref[rows, :][:, h:h + 1]
        kb = kh.astype(BF16)
        attn_ref[rows, sl] = (_dot_nt(qh.astype(BF16), kb) * dec).astype(BF16)
        qd_ref[rows, sl] = (qh * egc).astype(BF16)
        kd_ref[rows, sl] = (kh * ekc).astype(BF16)
        rk_ref[rows, sl] = (kh * (bcol * egc)).astype(BF16)
        rv_ref[rows, sl] = (vh * bcol).astype(BF16)
        return jnp.where(strict, bcol * _dot_nt(kb, kb) * dec, 0.0)

    for b0 in range(0, len(probs), GDN_BATCH):
        batch = probs[b0:b0 + GDN_BATCH]
        solve_batch(batch, [intra_chunk(c, h) for c, h in batch])

    for c in range(nchunk):
        rows = rows_of(c)
        st = [st_ref[h] for h in range(GDN_HEADS)]
        sb = [x.astype(BF16) for x in st]
        vn = [u_ref[rows, lanes_of(h)] - _dot(w_ref[rows, lanes_of(h)], sb[h]) for h in range(GDN_HEADS)]
        vnb = [x.astype(BF16) for x in vn]
        for h in range(GDN_HEADS):
            sl = lanes_of(h)
            st_ref[h] = st[h] * egl[c][:, h:h + 1] + _dot_tn(kd_ref[rows, sl], vnb[h])
            os_ref[rows, sl] = _dot(qd_ref[rows, sl], sb[h]) + _dot(attn_ref[rows, sl], vnb[h])

    nw = nw_ref[...]
    for h in range(GDN_HEADS):
        sl = slice(h * HEAD_DIM, (h + 1) * HEAD_DIM)
        o = os_ref[:, sl]
        zz = in_ref[:, COL_Z + h * HEAD_DIM:COL_Z + (h + 1) * HEAD_DIM]
        y = o * lax.rsqrt(jnp.mean(o * o, axis=-1, keepdims=True) + EPS) * nw * _silu(zz)
        o_ref[:, sl] = y.astype(o_ref.dtype)


def _gdn(proj, cw, alog, dtb, nw, *, batch, seq, ts):
    nblk = seq // ts
    small = pl.BlockSpec((1, 128), lambda b, s: (0, 0))
    return pl.pallas_call(
        _gdn_kernel,
        grid=(batch, nblk),
        in_specs=[
            pl.BlockSpec((ts, GDN_IN_WIDTH), lambda b, s: (b * nblk + s, 0)),
            pl.BlockSpec((GDN_CONV, 3 * GDN_WIDTH), lambda b, s: (0, 0)),
            small, small, small,
        ],
        out_specs=pl.BlockSpec((ts, GDN_WIDTH), lambda b, s: (b * nblk + s, 0)),
        out_shape=jax.ShapeDtypeStruct((batch * seq, GDN_WIDTH), BF16),
        scratch_shapes=(
            [pltpu.VMEM((ts + HALO, 3 * GDN_WIDTH), F32)]
            + [pltpu.VMEM((ts, GDN_WIDTH), F32)] * 3
            + [pltpu.VMEM((ts, 128), F32)] * 5
            + [pltpu.VMEM((ts, GDN_WIDTH), BF16)] * 6
            + [pltpu.VMEM((ts, GDN_WIDTH), F32)] * 2
            + [pltpu.VMEM((GDN_HEADS, HEAD_DIM, HEAD_DIM), F32)]
        ),
        compiler_params=_cparams(("arbitrary", "arbitrary")),
        name="gdn",
    )(proj, cw, alog, dtb, nw)


def _outproj_kernel(x_ref, yp_ref, yg_ref, yl_ref, w_ref, o_ref):
    mixed = jnp.concatenate([yp_ref[...], yg_ref[...], yl_ref[...]], axis=1)
    o_ref[...] = x_ref[...] + _dot(mixed, w_ref[...])


def _outproj(x, yp, yg, yl, w, layer, *, tm):
    t, d = x.shape

    def rows(width):
        return pl.BlockSpec((tm, width), lambda i: (i, 0))

    return pl.pallas_call(
        _outproj_kernel,
        grid=(t // tm,),
        in_specs=[rows(d), rows(POOL_WIDTH), rows(GDN_WIDTH), rows(LRU_WIDTH),
                  pl.BlockSpec((None, d, d), lambda i: (layer, 0, 0))],
        out_specs=rows(d),
        out_shape=jax.ShapeDtypeStruct((t, d), F32),
        compiler_params=_cparams(("arbitrary",)),
        name="outproj",
    )(x, yp, yg, yl, w)


def _ffn_up_kernel(x_ref, nw_ref, wg_ref, wv_ref, cw_ref, o_ref, h_ref, g_ref, carry_ref,
                   *, tiles_per_seq, rb):
    i = pl.program_id(0)
    f = pl.program_id(1)
    tm = x_ref.shape[0]

    def tile(first):
        halo = carry_ref[f]
        seq_start = (i % tiles_per_seq) == 0
        g_ref[0:HALO, :] = jnp.where(seq_start, jnp.zeros_like(halo), halo)
        cw = cw_ref[...]
        wg = wg_ref[...].astype(BF16)
        wv = wv_ref[...].astype(BF16)
        for r in range(tm // rb):
            rows = slice(r * rb, (r + 1) * rb)
            if first:
                hb = _rmsnorm(x_ref[rows, :], nw_ref[...]).astype(BF16)
                h_ref[rows, :] = hb
            else:
                hb = h_ref[rows, :]
            g_ref[HALO + r * rb:HALO + (r + 1) * rb, :] = _dot(hb, wg)
            val = _dot(hb, wv)
            conv = _conv_from_ext(g_ref.at[r * rb:r * rb + HALO + rb, :], 0, g_ref.shape[1], cw, FFN_CONV, rb)
            o_ref[rows, :] = (_gelu(conv) * val).astype(o_ref.dtype)
        carry_ref[f] = g_ref[tm:tm + HALO, :]

    @pl.when(f == 0)
    def _():
        tile(True)

    @pl.when(f > 0)
    def _():
        tile(False)


def _ffn_up(x, nw, w_up, cw, layer, *, seq, tm, tf, rb):
    t, d = x.shape
    nf = D_FF // tf
    kern = functools.partial(_ffn_up_kernel, tiles_per_seq=seq // tm, rb=rb)
    return pl.pallas_call(
        kern,
        grid=(t // tm, nf),
        in_specs=[
            pl.BlockSpec((tm, d), lambda i, f: (i, 0)),
            pl.BlockSpec((1, d), lambda i, f: (0, 0)),
            pl.BlockSpec((None, d, tf), lambda i, f: (layer, 0, f)),
            pl.BlockSpec((None, d, tf), lambda i, f: (layer, 0, nf + f)),
            pl.BlockSpec((None, FFN_CONV, tf), lambda i, f: (layer, 0, f)),
        ],
        out_specs=pl.BlockSpec((tm, tf), lambda i, f: (i, f)),
        out_shape=jax.ShapeDtypeStruct((t, D_FF), BF16),
        scratch_shapes=[
            pltpu.VMEM((tm, d), BF16),
            pltpu.VMEM((tm + HALO, tf), F32),
            pltpu.VMEM((nf, HALO, tf), F32),
        ],
        compiler_params=_cparams(("arbitrary", "arbitrary")),
        name="ffn_up",
    )(x, nw, w_up, w_up, cw)


def _ffn_down_kernel(a_ref, w_ref, x_ref, fnw_ref, o_ref, *, nj, final_norm):
    j = pl.program_id(1)
    tn = w_ref.shape[1]
    half = tn // 2
    a = a_ref[...]
    halves = [(c, x_ref[:, c:c + half] + _dot(a, w_ref[:, c:c + half])) for c in (0, half)]
    if not final_norm:
        for c, y in halves:
            o_ref[:, c:c + half] = y
        return
    for jj in range(nj):
        @pl.when(j == jj)
        def _():
            for c, y in halves:
                o_ref[:, jj * tn + c:jj * tn + c + half] = y

    @pl.when(j == nj - 1)
    def _():
        o_ref[...] = _rmsnorm(o_ref[...], fnw_ref[...])


def _ffn_down(act, w_down, x, fnw, layer, *, tm, tn, final_norm):
    t, d = x.shape
    nj = d // tn
    kern = functools.partial(_ffn_down_kernel, nj=nj, final_norm=final_norm)
    return pl.pallas_call(
        kern,
        grid=(t // tm, nj),
        in_specs=[
            pl.BlockSpec((tm, D_FF), lambda i, j: (i, 0)),
            pl.BlockSpec((None, D_FF, tn), lambda i, j: (layer, 0, j)),
            pl.BlockSpec((tm, tn), lambda i, j: (i, j)),
            pl.BlockSpec((1, d), lambda i, j: (0, 0)),
        ],
        out_specs=(pl.BlockSpec((tm, d), lambda i, j: (i, 0)) if final_norm
                   else pl.BlockSpec((tm, tn), lambda i, j: (i, j))),
        out_shape=jax.ShapeDtypeStruct((t, d), F32),
        compiler_params=_cparams(("arbitrary", "arbitrary")),
        name="ffn_down",
    )(act, w_down, x, fnw)


def _pad_cols(w, width):
    return jnp.pad(w, ((0, 0), (0, width - w.shape[1])))


def _prep_w_in(w, layer):
    wt = jnp.transpose(w, (2, 0, 1))[:, layer, :].astype(BF16)
    return jnp.pad(wt, ((0, PROJ_COLS - wt.shape[0]), (0, 0)))


TILES = dict(
    inproj=dict(tm=1024, tn=1792, norm_rows=256),
    pool=dict(tp=1024),
    gdn=dict(ts=2 * CHUNK),
    lru=dict(tl=256),
    outproj=dict(tm=512),
    ffn_up=dict(tm=1024, tf=768, rb=512),
    ffn_down=dict(tm=1024, tn=512),
    ffn_down_final=dict(tm=512, tn=1024),
)


def _row(v, width=None):
    v = v.reshape(1, -1).astype(F32)
    return v if width is None else _pad_cols(v, width)


def kernel(x, norm1_w, w_in, pool_w, pool_b, pool_scale, gdn_conv_w, gdn_a_log, gdn_dt_bias, gdn_norm_w,
           lru_conv_w, lru_conv_b, lru_wa, lru_ba, lru_wx, lru_bx, lru_lambda, w_out, norm2_w, ffn_up,
           ffn_conv_w, ffn_down, final_norm_w):
    batch, seq, d = x.shape
    xf = x.reshape(batch * seq, d)
    w_out_b = w_out.astype(BF16)
    w_down_b = ffn_down.astype(BF16)
    fnw = _row(final_norm_w)
    for l in range(DEPTH):
        proj = _inproj(xf, _row(norm1_w[l]), _prep_w_in(w_in, l), **TILES["inproj"])
        y_pool = _pool(proj, pool_w[l].astype(BF16), _row(pool_b[l]), _row(pool_scale[l]),
                       batch=batch, seq=seq, **TILES["pool"])
        y_gdn = _gdn(proj, gdn_conv_w[l], _row(gdn_a_log[l], 128), _row(gdn_dt_bias[l], 128),
                     _row(gdn_norm_w[l]), batch=batch, seq=seq, **TILES["gdn"])
        y_lru = _lru(proj, lru_conv_w[l], _row(lru_conv_b[l]), lru_wa[l].astype(BF16), _row(lru_ba[l]),
                     lru_wx[l].astype(BF16), _row(lru_bx[l]), _row(lru_lambda[l]),
                     batch=batch, seq=seq, **TILES["lru"])
        x1 = _outproj(xf, y_pool, y_gdn, y_lru, w_out_b, l, **TILES["outproj"])
        act = _ffn_up(x1, _row(norm2_w[l]), ffn_up, ffn_conv_w, l, seq=seq, **TILES["ffn_up"])
        last = l == DEPTH - 1
        xf = _ffn_down(act, w_down_b, x1, fnw, l, final_norm=last,
                       **TILES["ffn_down_final" if last else "ffn_down"])
    return xf.reshape(batch, seq, d)
```

```python
import functools

import jax
import jax.numpy as jnp
from jax import lax
from jax.experimental import pallas as pl
from jax.experimental.pallas import tpu as pltpu

F32 = jnp.float32
BF16 = jnp.bfloat16

D_MODEL = 2048
DEPTH = 2
POOL_WINDOWS = (2, 4, 8, 16)
POOL_GROUP_DIM = 128
POOL_WIDTH = 512
GDN_HEADS = 6
HEAD_DIM = 128
GDN_WIDTH = 768
GDN_CONV = 4
GDN_CHUNK = 64
LRU_BLOCKS = 6
LRU_WIDTH = 768
LRU_CONV = 4
LRU_C = 8.0
D_FF = 3 * D_MODEL
FFN_CONV = 3
EPS = 1e-6

LANES = 128
COL_POOL = 0
COL_Q, COL_K, COL_V, COL_Z = 512, 1280, 2048, 2816
COL_AB = 3584
COL_XR, COL_GR = 3596, 4364
PROJ_COLS = 5376
GDN_IN_WIDTH = COL_AB + LANES
LRU_IN_WIDTH = PROJ_COLS - COL_AB
assert COL_AB % LRU_IN_WIDTH == 0 and LRU_IN_WIDTH % LANES == 0 and GDN_IN_WIDTH % LANES == 0

SUBLANES = 8
HALO = SUBLANES
POOL_HALO = 2 * SUBLANES
assert POOL_HALO >= max(POOL_WINDOWS) - 1
CHUNK_SHIFT = 7
CHUNK = 1 << CHUNK_SHIFT
NEG_BIG = -1e30
GDN_BATCH = 12

V7X_VMEM_BYTES = 64 * 1024 * 1024
V7X_VMEM_LIMIT = V7X_VMEM_BYTES - 4 * 1024 * 1024


def _cparams(sem):
    return pltpu.CompilerParams(dimension_semantics=sem, vmem_limit_bytes=V7X_VMEM_LIMIT)


def _dot(a, b):
    return jnp.dot(a, b, preferred_element_type=F32)


def _dot_nt(a, b):
    return lax.dot_general(a, b, (((1,), (1,)), ((), ())), preferred_element_type=F32)


def _dot_tn(a, b):
    return lax.dot_general(a, b, (((0,), (0,)), ((), ())), preferred_element_type=F32)


def _sigmoid(x):
    return 0.5 + 0.5 * jnp.tanh(0.5 * x)


def _silu(x):
    h = 0.5 * x
    return h + h * jnp.tanh(h)


def _softplus(x):
    return jnp.maximum(x, 0.0) + jnp.log1p(jnp.exp(-jnp.abs(x)))


def _gelu(x):
    return jax.nn.gelu(x, approximate=True)


def _rmsnorm(x, w):
    return x * lax.rsqrt(jnp.mean(x * x, axis=-1, keepdims=True) + EPS) * w


def _conv_from_ext(ext_ref, col0, width, w, taps, rows):
    x = ext_ref[0:HALO + rows, col0:col0 + width]
    acc = x[HALO:] * w[taps - 1:taps, :]
    for lag in range(1, taps):
        shifted = pltpu.roll(x, lag, 0)[HALO:]
        acc = acc + shifted * w[taps - 1 - lag:taps - lag, :]
    return acc


def _inproj_kernel(x_ref, nw_ref, w_ref, o_ref, h_ref, *, norm_rows):
    j = pl.program_id(1)
    tm = x_ref.shape[0]

    @pl.when(j == 0)
    def _():
        w = w_ref[...]
        for r0 in range(0, tm, norm_rows):
            rows = slice(r0, r0 + norm_rows)
            h = _rmsnorm(x_ref[rows, :], nw_ref[...]).astype(BF16)
            h_ref[rows, :] = h
            o_ref[rows, :] = _dot_nt(h, w)

    @pl.when(j > 0)
    def _():
        o_ref[...] = _dot_nt(h_ref[...], w_ref[...])


def _inproj(x, nw, w, *, tm, tn, norm_rows):
    t, d = x.shape
    n = w.shape[0]
    return pl.pallas_call(
        functools.partial(_inproj_kernel, norm_rows=norm_rows),
        grid=(t // tm, n // tn),
        in_specs=[
            pl.BlockSpec((tm, d), lambda i, j: (i, 0)),
            pl.BlockSpec((1, d), lambda i, j: (0, 0)),
            pl.BlockSpec((tn, d), lambda i, j: (j, 0)),
        ],
        out_specs=pl.BlockSpec((tm, tn), lambda i, j: (i, j)),
        out_shape=jax.ShapeDtypeStruct((t, n), F32),
        scratch_shapes=[pltpu.VMEM((tm, d), BF16)],
        compiler_params=_cparams(("arbitrary", "arbitrary")),
        name="inproj",
    )(x, nw, w)


def _pool_kernel(u_ref, w_ref, b_ref, sc_ref, o_ref, ext_ref):
    s = pl.program_id(1)
    tp = u_ref.shape[0]

    @pl.when(s == 0)
    def _():
        ext_ref[0:POOL_HALO, :] = jnp.zeros((POOL_HALO, POOL_WIDTH), F32)

    @pl.when(s > 0)
    def _():
        ext_ref[0:POOL_HALO, :] = ext_ref[tp:tp + POOL_HALO, :]

    ext_ref[POOL_HALO:POOL_HALO + tp, :] = u_ref[...]
    pos = s * tp + lax.broadcasted_iota(jnp.int32, (tp, 1), 0)

    for g, win in enumerate(POOL_WINDOWS):
        sl = slice(g * POOL_GROUP_DIM, (g + 1) * POOL_GROUP_DIM)
        x = ext_ref[:, sl]
        acc = x
        span = 1
        while span < win:
            acc = acc + pltpu.roll(acc, span, 0)
            span *= 2
        wsum = acc[POOL_HALO:]
        u = x[POOL_HALO:]
        cnt = jnp.minimum(pos + 1, win).astype(F32)
        d = wsum / cnt - u
        y = _dot(d.astype(BF16), w_ref[g]) + b_ref[:, sl]
        o_ref[:, sl] = (y * sc_ref[:, sl]).astype(o_ref.dtype)


def _pool(proj, pw, pb, psc, *, batch, seq, tp):
    nblk = seq // tp
    return pl.pallas_call(
        _pool_kernel,
        grid=(batch, nblk),
        in_specs=[
            pl.BlockSpec((tp, POOL_WIDTH), lambda b, s: (b * nblk + s, COL_POOL // POOL_WIDTH)),
            pl.BlockSpec((4, 128, 128), lambda b, s: (0, 0, 0)),
            pl.BlockSpec((1, POOL_WIDTH), lambda b, s: (0, 0)),
            pl.BlockSpec((1, POOL_WIDTH), lambda b, s: (0, 0)),
        ],
        out_specs=pl.BlockSpec((tp, POOL_WIDTH), lambda b, s: (b * nblk + s, 0)),
        out_shape=jax.ShapeDtypeStruct((batch * seq, POOL_WIDTH), BF16),
        scratch_shapes=[pltpu.VMEM((tp + POOL_HALO, POOL_WIDTH), F32)],
        compiler_params=_cparams(("arbitrary", "arbitrary")),
        name="pool",
    )(proj, pw, pb, psc)


def _lru_kernel(in_ref, cw_ref, cb_ref, wa_ref, ba_ref, wx_ref, bx_ref, lam_ref,
                o_ref, ext_ref, gr_ref, hc_ref):
    s = pl.program_id(1)
    tl = in_ref.shape[0]

    @pl.when(s == 0)
    def _():
        ext_ref[0:HALO, :] = jnp.zeros((HALO, LRU_WIDTH), F32)
        hc_ref[...] = jnp.zeros_like(hc_ref)

    @pl.when(s > 0)
    def _():
        ext_ref[0:HALO, :] = ext_ref[tl:tl + HALO, :]

    aligned = pltpu.roll(in_ref[...], LRU_IN_WIDTH - (COL_XR - COL_AB), 1)
    ext_ref[HALO:HALO + tl, :] = aligned[:, 0:LRU_WIDTH]
    gr_ref[...] = aligned[:, LRU_WIDTH:2 * LRU_WIDTH]

    row = lax.broadcasted_iota(jnp.int32, (tl, 128), 0)
    first = (row == 0) & (s == 0)
    groups = tl // SUBLANES
    sub = lax.broadcasted_iota(jnp.int32, (groups, SUBLANES, 128), 1)
    nsp = -LRU_C * _softplus(-lam_ref[...])
    cw = cw_ref[...]

    for blk in range(LRU_BLOCKS):
        sl = slice(blk * 128, (blk + 1) * 128)
        xc = _conv_from_ext(ext_ref, blk * 128, 128, cw[:, sl], LRU_CONV, tl) + cb_ref[:, sl]
        xb = xc.astype(BF16)
        r = _sigmoid(_dot(xb, wa_ref[blk]) + ba_ref[:, sl])
        i = _sigmoid(_dot(xb, wx_ref[blk]) + bx_ref[:, sl])
        log_a = r * nsp[:, sl]
        a = jnp.exp(log_a)
        mult = jnp.sqrt(-jnp.tanh(log_a) * (a * a + 1.0))
        mult = jnp.where(first, 1.0, mult)
        bv = mult * i * xc
        a3 = a.reshape(groups, SUBLANES, 128)
        b3 = bv.reshape(groups, SUBLANES, 128)
        sh = 1
        while sh < SUBLANES:
            keep = sub >= sh
            a_sh = jnp.where(keep, pltpu.roll(a3, sh, 1), 1.0)
            b_sh = jnp.where(keep, pltpu.roll(b3, sh, 1), 0.0)
            b3 = a3 * b_sh + b3
            a3 = a3 * a_sh
            sh *= 2
        hprev = hc_ref[:, sl]
        hs = []
        for g in range(groups):
            hg = a3[g] * hprev + b3[g]
            hs.append(hg)
            hprev = hg[SUBLANES - 1:SUBLANES, :]
        hc_ref[:, sl] = hprev
        h = jnp.concatenate(hs, axis=0)
        o_ref[:, sl] = (h * _gelu(gr_ref[:, sl])).astype(o_ref.dtype)


def _lru(proj, cw, cb, wa, ba, wx, bx, lam, *, batch, seq, tl):
    nblk = seq // tl
    vec = pl.BlockSpec((1, LRU_WIDTH), lambda b, s: (0, 0))
    mat = pl.BlockSpec((LRU_BLOCKS, 128, 128), lambda b, s: (0, 0, 0))
    return pl.pallas_call(
        _lru_kernel,
        grid=(batch, nblk),
        in_specs=[
            pl.BlockSpec((tl, LRU_IN_WIDTH), lambda b, s: (b * nblk + s, COL_AB // LRU_IN_WIDTH)),
            pl.BlockSpec((LRU_CONV, LRU_WIDTH), lambda b, s: (0, 0)),
            vec, mat, vec, mat, vec, vec,
        ],
        out_specs=pl.BlockSpec((tl, LRU_WIDTH), lambda b, s: (b * nblk + s, 0)),
        out_shape=jax.ShapeDtypeStruct((batch * seq, LRU_WIDTH), BF16),
        scratch_shapes=[pltpu.VMEM((tl + HALO, LRU_WIDTH), F32), pltpu.VMEM((tl, LRU_WIDTH), F32),
                        pltpu.VMEM((1, LRU_WIDTH), F32)],
        compiler_params=_cparams(("arbitrary", "arbitrary")),
        name="lru",
    )(proj, cw, cb, wa, ba, wx, bx, lam)


def _unit_lower_inverses(a_list, eye, level_masks):
    m8 = level_masks[0]
    zero = jnp.zeros((), BF16)
    ab = [a.astype(BF16) for a in a_list]
    n1 = [jnp.where(m8, -a, zero) for a in ab]
    p = [eye + n.astype(F32) for n in n1]
    n2 = [_dot(n, n).astype(BF16) for n in n1]
    p = [pi + _dot(pi.astype(BF16), n) for pi, n in zip(p, n2)]
    n4 = [_dot(n, n).astype(BF16) for n in n2]
    p = [pi + _dot(pi.astype(BF16), n) for pi, n in zip(p, n4)]
    for lvl in range(len(level_masks)):
        inner = level_masks[lvl]
        sel = ~inner if lvl + 1 == len(level_masks) else (level_masks[lvl + 1] & ~inner)
        pb = [pi.astype(BF16) for pi in p]
        t = [_dot(pbi, jnp.where(sel, a, zero)).astype(BF16) for pbi, a in zip(pb, ab)]
        p = [pi - _dot(ti, pbi) for pi, ti, pbi in zip(p, t, pb)]
    return p


def _gdn_kernel(in_ref, cw_ref, alog_ref, dtb_ref, nw_ref,
                o_ref, ext_ref, qs_ref, ks_ref, vs_ref, gc_ref, beta_ref, gt_ref, eg_ref, ekd_ref,
                attn_ref, qd_ref, kd_ref, rk_ref, rv_ref, w_ref, u_ref, os_ref, st_ref):
    s = pl.program_id(1)
    ts = in_ref.shape[0]

    @pl.when(s == 0)
    def _():
        ext_ref[0:HALO, :] = jnp.zeros((HALO, 3 * GDN_WIDTH), F32)
        st_ref[...] = jnp.zeros_like(st_ref)

    @pl.when(s > 0)
    def _():
        ext_ref[0:HALO, :] = ext_ref[ts:ts + HALO, :]

    ext_ref[HALO:HALO + ts, :] = in_ref[:, COL_Q:COL_Z]

    cw = cw_ref[...]
    for h in range(GDN_HEADS):
        for ti, dst in enumerate((qs_ref, ks_ref, vs_ref)):
            c0 = ti * GDN_WIDTH + h * HEAD_DIM
            t = _conv_from_ext(ext_ref, c0, HEAD_DIM, cw[:, c0:c0 + HEAD_DIM], GDN_CONV, ts)
            t = _silu(t)
            if ti < 2:
                t = t * lax.rsqrt(jnp.sum(t * t, axis=-1, keepdims=True) + EPS)
            if ti == 0:
                t = t * (HEAD_DIM ** -0.5)
            dst[:, h * HEAD_DIM:(h + 1) * HEAD_DIM] = t

    ab = in_ref[:, COL_AB:COL_AB + LANES]
    graw = -jnp.exp(alog_ref[...]) * _softplus(ab + dtb_ref[...])
    beta_ref[...] = _sigmoid(ab)
    rr = lax.broadcasted_iota(jnp.int32, (ts, ts), 0)
    cc = lax.broadcasted_iota(jnp.int32, (ts, ts), 1)
    ltri = ((rr >= cc) & ((rr >> CHUNK_SHIFT) == (cc >> CHUNK_SHIFT))).astype(BF16)
    g1 = graw.astype(BF16)
    r1 = graw - g1.astype(F32)
    g2 = r1.astype(BF16)
    g3 = (r1 - g2.astype(F32)).astype(BF16)
    gc_ref[...] = _dot(ltri, g1) + _dot(ltri, g2) + _dot(ltri, g3)

    ri = lax.broadcasted_iota(jnp.int32, (CHUNK, CHUNK), 0)
    ci = lax.broadcasted_iota(jnp.int32, (CHUNK, CHUNK), 1)
    causal = ri >= ci
    strict = ri > ci
    eye = (ri == ci).astype(F32)
    level_masks = tuple((ri >> sh) == (ci >> sh) for sh in (3, 4, 5, 6))
    nchunk = ts // CHUNK
    probs = [(c, h) for c in range(nchunk) for h in range(GDN_HEADS)]

    def rows_of(c):
        return slice(c * CHUNK, (c + 1) * CHUNK)

    def lanes_of(h):
        return slice(h * HEAD_DIM, (h + 1) * HEAD_DIM)

    def solve_batch(batch, a_batch):
        for (c, h), mi in zip(batch, _unit_lower_inverses(a_batch, eye, level_masks)):
            rows, sl = rows_of(c), lanes_of(h)
            mb = mi.astype(BF16)
            w_ref[rows, sl] = _dot(mb, rk_ref[rows, sl]).astype(BF16)
            u_ref[rows, sl] = _dot(mb, rv_ref[rows, sl])

    egl = []
    for c in range(nchunk):
        gcb = gc_ref[rows_of(c), :]
        gt_ref[rows_of(c), :] = gcb.T
        gl = gcb[CHUNK - 1:CHUNK, :]
        eg_ref[rows_of(c), :] = jnp.exp(gcb)
        ekd_ref[rows_of(c), :] = jnp.exp(gl - gcb)
        egl.append(jnp.exp(gl))

    def intra_chunk(c, h):
        rows, sl = rows_of(c), lanes_of(h)
        qh = qs_ref[rows, sl]
        kh = ks_ref[rows, sl]
        vh = vs_ref[rows, sl]
        gcol = gc_ref[rows, :][:, h:h + 1]
        grow = gt_ref[c * CHUNK + h:c * CHUNK + h + 1, :]
        dec = jnp.exp(jnp.where(causal, gcol - grow, NEG_BIG))
        bcol = beta_ref[rows, :][:, GDN_HEADS + h:GDN_HEADS + h + 1]
        egc = eg_ref[rows, :][:, h:h + 1]
        ekc = ekd_ref[rows, :][:, h:h + 1]
        kb = kh.astype(BF16)
        attn_ref[rows, sl] = (_dot_nt(qh.astype(BF16), kb) * dec).astype(BF16)
        qd_ref[rows, sl] = (qh * egc).astype(BF16)
        kd_ref[rows, sl] = (kh * ekc).astype(BF16)
        rk_ref[rows, sl] = (kh * (bcol * egc)).astype(BF16)
        rv_ref[rows, sl] = (vh * bcol).astype(BF16)
        return jnp.where(strict, bcol * _dot_nt(kb, kb) * dec, 0.0)

    for b0 in range(0, len(probs), GDN_BATCH):
        batch = probs[b0:b0 + GDN_BATCH]
        solve_batch(batch, [intra_chunk(c, h) for c, h in batch])

    for c in range(nchunk):
        rows = rows_of(c)
        st = [st_ref[h] for h in range(GDN_HEADS)]
        sb = [x.astype(BF16) for x in st]
        vn = [u_ref[rows, lanes_of(h)] - _dot(w_ref[rows, lanes_of(h)], sb[h]) for h in range(GDN_HEADS)]
        vnb = [x.astype(BF16) for x in vn]
        for h in range(GDN_HEADS):
            sl = lanes_of(h)
            st_ref[h] = st[h] * egl[c][:, h:h + 1] + _dot_tn(kd_ref[rows, sl], vnb[h])
            os_ref[rows, sl] = _dot(qd_ref[rows, sl], sb[h]) + _dot(attn_ref[rows, sl], vnb[h])

    nw = nw_ref[...]
    for h in range(GDN_HEADS):
        sl = slice(h * HEAD_DIM, (h + 1) * HEAD_DIM)
        o = os_ref[:, sl]
        zz = in_ref[:, COL_Z + h * HEAD_DIM:COL_Z + (h + 1) * HEAD_DIM]
        y = o * lax.rsqrt(jnp.mean(o * o, axis=-1, keepdims=True) + EPS) * nw * _silu(zz)
        o_ref[:, sl] = y.astype(o_ref.dtype)


def _gdn(proj, cw, alog, dtb, nw, *, batch, seq, ts):
    nblk = seq // ts
    small = pl.BlockSpec((1, 128), lambda b, s: (0, 0))
    return pl.pallas_call(
        _gdn_kernel,
        grid=(batch, nblk),
        in_specs=[
            pl.BlockSpec((ts, GDN_IN_WIDTH), lambda b, s: (b * nblk + s, 0)),
            pl.BlockSpec((GDN_CONV, 3 * GDN_WIDTH), lambda b, s: (0, 0)),
            small, small, small,
        ],
        out_specs=pl.BlockSpec((ts, GDN_WIDTH), lambda b, s: (b * nblk + s, 0)),
        out_shape=jax.ShapeDtypeStruct((batch * seq, GDN_WIDTH), BF16),
        scratch_shapes=(
            [pltpu.VMEM((ts + HALO, 3 * GDN_WIDTH), F32)]
            + [pltpu.VMEM((ts, GDN_WIDTH), F32)] * 3
            + [pltpu.VMEM((ts, 128), F32)] * 5
            + [pltpu.VMEM((ts, GDN_WIDTH), BF16)] * 6
            + [pltpu.VMEM((ts, GDN_WIDTH), F32)] * 2
            + [pltpu.VMEM((GDN_HEADS, HEAD_DIM, HEAD_DIM), F32)]
        ),
        compiler_params=_cparams(("arbitrary", "arbitrary")),
        name="gdn",
    )(proj, cw, alog, dtb, nw)


def _outproj_kernel(x_ref, yp_ref, yg_ref, yl_ref, w_ref, o_ref):
    mixed = jnp.concatenate([yp_ref[...], yg_ref[...], yl_ref[...]], axis=1)
    o_ref[...] = x_ref[...] + _dot(mixed, w_ref[...])


def _outproj(x, yp, yg, yl, w, layer, *, tm):
    t, d = x.shape

    def rows(width):
        return pl.BlockSpec((tm, width), lambda i: (i, 0))

    return pl.pallas_call(
        _outproj_kernel,
        grid=(t // tm,),
        in_specs=[rows(d), rows(POOL_WIDTH), rows(GDN_WIDTH), rows(LRU_WIDTH),
                  pl.BlockSpec((None, d, d), lambda i: (layer, 0, 0))],
        out_specs=rows(d),
        out_shape=jax.ShapeDtypeStruct((t, d), F32),
        compiler_params=_cparams(("arbitrary",)),
        name="outproj",
    )(x, yp, yg, yl, w)


def _ffn_up_kernel(x_ref, nw_ref, wg_ref, wv_ref, cw_ref, o_ref, h_ref, g_ref, carry_ref,
                   *, tiles_per_seq, rb):
    i = pl.program_id(0)
    f = pl.program_id(1)
    tm = x_ref.shape[0]

    @pl.when(f == 0)
    def _():
        h_ref[...] = _rmsnorm(x_ref[...], nw_ref[...]).astype(BF16)

    halo = carry_ref[f]
    seq_start = (i % tiles_per_seq) == 0
    g_ref[0:HALO, :] = jnp.where(seq_start, jnp.zeros_like(halo), halo)
    cw = cw_ref[...]
    wg = wg_ref[...].astype(BF16)
    wv = wv_ref[...].astype(BF16)
    for r in range(tm // rb):
        rows = slice(r * rb, (r + 1) * rb)
        hb = h_ref[rows, :]
        g_ref[HALO + r * rb:HALO + (r + 1) * rb, :] = _dot(hb, wg)
        val = _dot(hb, wv)
        conv = _conv_from_ext(g_ref.at[r * rb:r * rb + HALO + rb, :], 0, g_ref.shape[1], cw, FFN_CONV, rb)
        o_ref[rows, :] = (_gelu(conv) * val).astype(o_ref.dtype)
    carry_ref[f] = g_ref[tm:tm + HALO, :]


def _ffn_up(x, nw, w_up, cw, layer, *, seq, tm, tf, rb):
    t, d = x.shape
    nf = D_FF // tf
    kern = functools.partial(_ffn_up_kernel, tiles_per_seq=seq // tm, rb=rb)
    return pl.pallas_call(
        kern,
        grid=(t // tm, nf),
        in_specs=[
            pl.BlockSpec((tm, d), lambda i, f: (i, 0)),
            pl.BlockSpec((1, d), lambda i, f: (0, 0)),
            pl.BlockSpec((None, d, tf), lambda i, f: (layer, 0, f)),
            pl.BlockSpec((None, d, tf), lambda i, f: (layer, 0, nf + f)),
            pl.BlockSpec((None, FFN_CONV, tf), lambda i, f: (layer, 0, f)),
        ],
        out_specs=pl.BlockSpec((tm, tf), lambda i, f: (i, f)),
        out_shape=jax.ShapeDtypeStruct((t, D_FF), BF16),
        scratch_shapes=[
            pltpu.VMEM((tm, d), BF16),
            pltpu.VMEM((tm + HALO, tf), F32),
            pltpu.VMEM((nf, HALO, tf), F32),
        ],
        compiler_params=_cparams(("arbitrary", "arbitrary")),
        name="ffn_up",
    )(x, nw, w_up, w_up, cw)


def _ffn_down_kernel(a_ref, w_ref, x_ref, fnw_ref, o_ref, *, nj, final_norm):
    j = pl.program_id(1)
    tn = w_ref.shape[1]
    half = tn // 2
    a = a_ref[...]
    halves = [(c, x_ref[:, c:c + half] + _dot(a, w_ref[:, c:c + half])) for c in (0, half)]
    if not final_norm:
        for c, y in halves:
            o_ref[:, c:c + half] = y
        return
    for jj in range(nj):
        @pl.when(j == jj)
        def _():
            for c, y in halves:
                o_ref[:, jj * tn + c:jj * tn + c + half] = y

    @pl.when(j == nj - 1)
    def _():
        o_ref[...] = _rmsnorm(o_ref[...], fnw_ref[...])


def _ffn_down(act, w_down, x, fnw, layer, *, tm, tn, final_norm):
    t, d = x.shape
    nj = d // tn
    kern = functools.partial(_ffn_down_kernel, nj=nj, final_norm=final_norm)
    return pl.pallas_call(
        kern,
        grid=(t // tm, nj),
        in_specs=[
            pl.BlockSpec((tm, D_FF), lambda i, j: (i, 0)),
            pl.BlockSpec((None, D_FF, tn), lambda i, j: (layer, 0, j)),
            pl.BlockSpec((tm, tn), lambda i, j: (i, j)),
            pl.BlockSpec((1, d), lambda i, j: (0, 0)),
        ],
        out_specs=(pl.BlockSpec((tm, d), lambda i, j: (i, 0)) if final_norm
                   else pl.BlockSpec((tm, tn), lambda i, j: (i, j))),
        out_shape=jax.ShapeDtypeStruct((t, d), F32),
        compiler_params=_cparams(("arbitrary", "arbitrary")),
        name="ffn_down",
    )(act, w_down, x, fnw)


def _pad_cols(w, width):
    return jnp.pad(w, ((0, 0), (0, width - w.shape[1])))


def _prep_w_in(w, layer):
    wt = w[layer].T
    return jnp.pad(wt, ((0, PROJ_COLS - wt.shape[0]), (0, 0))).astype(BF16)


TILES = dict(
    inproj=dict(tm=1024, tn=1792, norm_rows=256),
    pool=dict(tp=1024),
    gdn=dict(ts=2 * CHUNK),
    lru=dict(tl=512),
    outproj=dict(tm=512),
    ffn_up=dict(tm=1024, tf=768, rb=512),
    ffn_down=dict(tm=1024, tn=512),
    ffn_down_final=dict(tm=512, tn=1024),
)


def _row(v, width=None):
    v = v.reshape(1, -1).astype(F32)
    return v if width is None else _pad_cols(v, width)


def kernel(x, norm1_w, w_in, pool_w, pool_b, pool_scale, gdn_conv_w, gdn_a_log, gdn_dt_bias, gdn_norm_w,
           lru_conv_w, lru_conv_b, lru_wa, lru_ba, lru_wx, lru_bx, lru_lambda, w_out, norm2_w, ffn_up,
           ffn_conv_w, ffn_down, final_norm_w):
    batch, seq, d = x.shape
    xf = x.reshape(batch * seq, d)
    w_out_b = w_out.astype(BF16)
    w_down_b = ffn_down.astype(BF16)
    fnw = _row(final_norm_w)
    for l in range(DEPTH):
        proj = _inproj(xf, _row(norm1_w[l]), _prep_w_in(w_in, l), **TILES["inproj"])
        y_pool = _pool(proj, pool_w[l].astype(BF16), _row(pool_b[l]), _row(pool_scale[l]),
                       batch=batch, seq=seq, **TILES["pool"])
        y_gdn = _gdn(proj, gdn_conv_w[l], _row(gdn_a_log[l], 128), _row(gdn_dt_bias[l], 128),
                     _row(gdn_norm_w[l]), batch=batch, seq=seq, **TILES["gdn"])
        y_lru = _lru(proj, lru_conv_w[l], _row(lru_conv_b[l]), lru_wa[l].astype(BF16), _row(lru_ba[l]),
                     lru_wx[l].astype(BF16), _row(lru_bx[l]), _row(lru_lambda[l]),
                     batch=batch, seq=seq, **TILES["lru"])
        x1 = _outproj(xf, y_pool, y_gdn, y_lru, w_out_b, l, **TILES["outproj"])
        act = _ffn_up(x1, _row(norm2_w[l]), ffn_up, ffn_conv_w, l, seq=seq, **TILES["ffn_up"])
        last = l == DEPTH - 1
        xf = _ffn_down(act, w_down_b, x1, fnw, l, final_norm=last,
                       **TILES["ffn_down_final" if last else "ffn_down"])
    return xf.reshape(batch, seq, d)
```

```python
import functools

import jax
import jax.numpy as jnp
from jax import lax
from jax.experimental import pallas as pl
from jax.experimental.pallas import tpu as pltpu

F32 = jnp.float32
BF16 = jnp.bfloat16

D_MODEL = 2048
DEPTH = 2
POOL_WINDOWS = (2, 4, 8, 16)
POOL_GROUP_DIM = 128
POOL_WIDTH = 512
GDN_HEADS = 6
HEAD_DIM = 128
GDN_WIDTH = 768
GDN_CONV = 4
LRU_BLOCKS = 6
LRU_WIDTH = 768
LRU_CONV = 4
LRU_C = 8.0
D_FF = 3 * D_MODEL
FFN_CONV = 3
EPS = 1e-6

LANES = 128
COL_POOL = 0
COL_Q, COL_Z = 512, 2816
COL_AB = 3584
COL_XR = 3596
PROJ_COLS = 5376
GDN_IN_WIDTH = COL_AB + LANES
LRU_IN_WIDTH = PROJ_COLS - COL_AB
assert COL_AB % LRU_IN_WIDTH == 0 and LRU_IN_WIDTH % LANES == 0 and GDN_IN_WIDTH % LANES == 0

SUBLANES = 8
HALO = SUBLANES
POOL_HALO = 2 * SUBLANES
assert POOL_HALO >= max(POOL_WINDOWS) - 1
CHUNK_SHIFT = 7
CHUNK = 1 << CHUNK_SHIFT
NEG_BIG = -1e30
GDN_BATCH = 12

V7X_VMEM_BYTES = 64 * 1024 * 1024
V7X_VMEM_LIMIT = V7X_VMEM_BYTES - 4 * 1024 * 1024


def _cparams(sem):
    return pltpu.CompilerParams(dimension_semantics=sem, vmem_limit_bytes=V7X_VMEM_LIMIT)


def _dot(a, b):
    return jnp.dot(a, b, preferred_element_type=F32)


def _dot_nt(a, b):
    return lax.dot_general(a, b, (((1,), (1,)), ((), ())), preferred_element_type=F32)


def _dot_tn(a, b):
    return lax.dot_general(a, b, (((0,), (0,)), ((), ())), preferred_element_type=F32)


def _sigmoid(x):
    return 0.5 + 0.5 * jnp.tanh(0.5 * x)


def _silu(x):
    h = 0.5 * x
    return h + h * jnp.tanh(h)


def _softplus(x):
    return jnp.maximum(x, 0.0) + jnp.log1p(jnp.exp(-jnp.abs(x)))


def _gelu(x):
    return jax.nn.gelu(x, approximate=True)


def _rmsnorm(x, w):
    return x * lax.rsqrt(jnp.mean(x * x, axis=-1, keepdims=True) + EPS) * w


def _conv_from_ext(ext_ref, col0, width, w, taps, rows):
    x = ext_ref[0:HALO + rows, col0:col0 + width]

    def tap(lag):
        return w[taps - 1 - lag:taps - lag, :]

    if taps == 4:
        x1 = pltpu.roll(x, 1, 0)
        y01 = x * tap(0) + x1 * tap(1)
        y23 = x * tap(2) + x1 * tap(3)
        return y01[HALO:] + pltpu.roll(y23, 2, 0)[HALO:]
    acc = x[HALO:] * tap(0)
    for lag in range(1, taps):
        shifted = pltpu.roll(x, lag, 0)[HALO:]
        acc = acc + shifted * tap(lag)
    return acc


def _inproj_kernel(x_ref, nw_ref, w_ref, o_ref, h_ref, *, norm_rows):
    j = pl.program_id(1)
    tm = x_ref.shape[0]

    @pl.when(j == 0)
    def _():
        w = w_ref[...]
        for r0 in range(0, tm, norm_rows):
            rows = slice(r0, r0 + norm_rows)
            h = _rmsnorm(x_ref[rows, :], nw_ref[...]).astype(BF16)
            h_ref[rows, :] = h
            o_ref[rows, :] = _dot_nt(h, w)

    @pl.when(j > 0)
    def _():
        o_ref[...] = _dot_nt(h_ref[...], w_ref[...])


def _inproj(x, nw, w, *, tm, tn, norm_rows):
    t, d = x.shape
    n = w.shape[0]
    return pl.pallas_call(
        functools.partial(_inproj_kernel, norm_rows=norm_rows),
        grid=(t // tm, n // tn),
        in_specs=[
            pl.BlockSpec((tm, d), lambda i, j: (i, 0)),
            pl.BlockSpec((1, d), lambda i, j: (0, 0)),
            pl.BlockSpec((tn, d), lambda i, j: (j, 0)),
        ],
        out_specs=pl.BlockSpec((tm, tn), lambda i, j: (i, j)),
        out_shape=jax.ShapeDtypeStruct((t, n), F32),
        scratch_shapes=[pltpu.VMEM((tm, d), BF16)],
        compiler_params=_cparams(("arbitrary", "arbitrary")),
        name="inproj",
    )(x, nw, w)


def _pool_kernel(u_ref, w_ref, b_ref, sc_ref, o_ref, ext_ref):
    s = pl.program_id(1)
    tp = u_ref.shape[0]

    @pl.when(s == 0)
    def _():
        ext_ref[0:POOL_HALO, :] = jnp.zeros((POOL_HALO, POOL_WIDTH), F32)

    @pl.when(s > 0)
    def _():
        ext_ref[0:POOL_HALO, :] = ext_ref[tp:tp + POOL_HALO, :]

    ext_ref[POOL_HALO:POOL_HALO + tp, :] = u_ref[...]
    pos = s * tp + lax.broadcasted_iota(jnp.int32, (tp, 1), 0)

    for g, win in enumerate(POOL_WINDOWS):
        sl = slice(g * POOL_GROUP_DIM, (g + 1) * POOL_GROUP_DIM)
        x = ext_ref[:, sl]
        acc = x
        span = 1
        while span < win:
            acc = acc + pltpu.roll(acc, span, 0)
            span *= 2
        wsum = acc[POOL_HALO:]
        u = x[POOL_HALO:]
        cnt = jnp.minimum(pos + 1, win).astype(F32)
        d = wsum / cnt - u
        y = _dot(d.astype(BF16), w_ref[g]) + b_ref[:, sl]
        o_ref[:, sl] = (y * sc_ref[:, sl]).astype(o_ref.dtype)


def _pool(proj, pw, pb, psc, *, batch, seq, tp):
    nblk = seq // tp
    return pl.pallas_call(
        _pool_kernel,
        grid=(batch, nblk),
        in_specs=[
            pl.BlockSpec((tp, POOL_WIDTH), lambda b, s: (b * nblk + s, COL_POOL // POOL_WIDTH)),
            pl.BlockSpec((4, 128, 128), lambda b, s: (0, 0, 0)),
            pl.BlockSpec((1, POOL_WIDTH), lambda b, s: (0, 0)),
            pl.BlockSpec((1, POOL_WIDTH), lambda b, s: (0, 0)),
        ],
        out_specs=pl.BlockSpec((tp, POOL_WIDTH), lambda b, s: (b * nblk + s, 0)),
        out_shape=jax.ShapeDtypeStruct((batch * seq, POOL_WIDTH), BF16),
        scratch_shapes=[pltpu.VMEM((tp + POOL_HALO, POOL_WIDTH), F32)],
        compiler_params=_cparams(("arbitrary", "arbitrary")),
        name="pool",
    )(proj, pw, pb, psc)


def _lru_kernel(in_ref, cw_ref, cb_ref, wa_ref, ba_ref, wx_ref, bx_ref, lam_ref,
                o_ref, ext_ref, gr_ref, hc_ref):
    s = pl.program_id(1)
    tl = in_ref.shape[0]

    @pl.when(s == 0)
    def _():
        ext_ref[0:HALO, :] = jnp.zeros((HALO, LRU_WIDTH), F32)
        hc_ref[...] = jnp.zeros_like(hc_ref)

    @pl.when(s > 0)
    def _():
        ext_ref[0:HALO, :] = ext_ref[tl:tl + HALO, :]

    aligned = pltpu.roll(in_ref[...], LRU_IN_WIDTH - (COL_XR - COL_AB), 1)
    ext_ref[HALO:HALO + tl, :] = aligned[:, 0:LRU_WIDTH]
    gr_ref[...] = aligned[:, LRU_WIDTH:2 * LRU_WIDTH]

    row = lax.broadcasted_iota(jnp.int32, (tl, 128), 0)
    first = (row == 0) & (s == 0)
    groups = tl // SUBLANES
    sub = lax.broadcasted_iota(jnp.int32, (groups, SUBLANES, 128), 1)
    nsp = -LRU_C * _softplus(-lam_ref[...])
    cw = cw_ref[...]

    for blk in range(LRU_BLOCKS):
        sl = slice(blk * 128, (blk + 1) * 128)
        xc = _conv_from_ext(ext_ref, blk * 128, 128, cw[:, sl], LRU_CONV, tl) + cb_ref[:, sl]
        xb = xc.astype(BF16)
        r = _sigmoid(_dot(xb, wa_ref[blk]) + ba_ref[:, sl])
        i = _sigmoid(_dot(xb, wx_ref[blk]) + bx_ref[:, sl])
        log_a = r * nsp[:, sl]
        a = jnp.exp(log_a)
        mult = jnp.sqrt(-jnp.tanh(log_a) * (a * a + 1.0))
        mult = jnp.where(first, 1.0, mult)
        bv = mult * i * xc
        a3 = a.reshape(groups, SUBLANES, 128)
        b3 = bv.reshape(groups, SUBLANES, 128)
        sh = 1
        while sh < SUBLANES:
            keep = sub >= sh
            a_sh = jnp.where(keep, pltpu.roll(a3, sh, 1), 1.0)
            b_sh = jnp.where(keep, pltpu.roll(b3, sh, 1), 0.0)
            b3 = a3 * b_sh + b3
            a3 = a3 * a_sh
            sh *= 2
        hprev = hc_ref[:, sl]
        hs = []
        for g in range(groups):
            hg = a3[g] * hprev + b3[g]
            hs.append(hg)
            hprev = hg[SUBLANES - 1:SUBLANES, :]
        hc_ref[:, sl] = hprev
        h = jnp.concatenate(hs, axis=0)
        o_ref[:, sl] = (h * _gelu(gr_ref[:, sl])).astype(o_ref.dtype)


def _lru(proj, cw, cb, wa, ba, wx, bx, lam, *, batch, seq, tl):
    nblk = seq // tl
    vec = pl.BlockSpec((1, LRU_WIDTH), lambda b, s: (0, 0))
    mat = pl.BlockSpec((LRU_BLOCKS, 128, 128), lambda b, s: (0, 0, 0))
    return pl.pallas_call(
        _lru_kernel,
        grid=(batch, nblk),
        in_specs=[
            pl.BlockSpec((tl, LRU_IN_WIDTH), lambda b, s: (b * nblk + s, COL_AB // LRU_IN_WIDTH)),
            pl.BlockSpec((LRU_CONV, LRU_WIDTH), lambda b, s: (0, 0)),
            vec, mat, vec, mat, vec, vec,
        ],
        out_specs=pl.BlockSpec((tl, LRU_WIDTH), lambda b, s: (b * nblk + s, 0)),
        out_shape=jax.ShapeDtypeStruct((batch * seq, LRU_WIDTH), BF16),
        scratch_shapes=[pltpu.VMEM((tl + HALO, LRU_WIDTH), F32), pltpu.VMEM((tl, LRU_WIDTH), F32),
                        pltpu.VMEM((1, LRU_WIDTH), F32)],
        compiler_params=_cparams(("arbitrary", "arbitrary")),
        name="lru",
    )(proj, cw, cb, wa, ba, wx, bx, lam)


def _unit_lower_inverses(a_list, eye, level_masks):
    m8 = level_masks[0]
    zero = jnp.zeros((), BF16)
    ab = [a.astype(BF16) for a in a_list]
    n1 = [jnp.where(m8, -a, zero) for a in ab]
    p = [eye + n.astype(F32) for n in n1]
    n2 = [_dot(n, n).astype(BF16) for n in n1]
    p = [pi + _dot(pi.astype(BF16), n) for pi, n in zip(p, n2)]
    n4 = [_dot(n, n).astype(BF16) for n in n2]
    p = [pi + _dot(pi.astype(BF16), n) for pi, n in zip(p, n4)]
    for lvl in range(len(level_masks)):
        inner = level_masks[lvl]
        sel = ~inner if lvl + 1 == len(level_masks) else (level_masks[lvl + 1] & ~inner)
        pb = [pi.astype(BF16) for pi in p]
        t = [_dot(pbi, jnp.where(sel, a, zero)).astype(BF16) for pbi, a in zip(pb, ab)]
        p = [pi - _dot(ti, pbi) for pi, ti, pbi in zip(p, t, pb)]
    return p


def _gdn_kernel(in_ref, cw_ref, alog_ref, dtb_ref, nw_ref,
                o_ref, ext_ref, qs_ref, ks_ref, vs_ref, gc_ref, beta_ref, gt_ref, eg_ref, ekd_ref,
                attn_ref, qd_ref, kd_ref, rk_ref, rv_ref, w_ref, u_ref, os_ref, st_ref):
    s = pl.program_id(1)
    ts = in_ref.shape[0]

    @pl.when(s == 0)
    def _():
        ext_ref[0:HALO, :] = jnp.zeros((HALO, 3 * GDN_WIDTH), F32)
        st_ref[...] = jnp.zeros_like(st_ref)

    @pl.when(s > 0)
    def _():
        ext_ref[0:HALO, :] = ext_ref[ts:ts + HALO, :]

    ext_ref[HALO:HALO + ts, :] = in_ref[:, COL_Q:COL_Z]

    cw = cw_ref[...]
    for h in range(GDN_HEADS):
        for ti, dst in enumerate((qs_ref, ks_ref, vs_ref)):
            c0 = ti * GDN_WIDTH + h * HEAD_DIM
            t = _conv_from_ext(ext_ref, c0, HEAD_DIM, cw[:, c0:c0 + HEAD_DIM], GDN_CONV, ts)
            t = _silu(t)
            if ti < 2:
                inv = lax.rsqrt(jnp.sum(t * t, axis=-1, keepdims=True) + EPS)
                if ti == 0:
                    inv = inv * (HEAD_DIM ** -0.5)
                t = t * inv
            dst[:, h * HEAD_DIM:(h + 1) * HEAD_DIM] = t

    ab = in_ref[:, COL_AB:COL_AB + LANES]
    graw = -jnp.exp(alog_ref[...]) * _softplus(ab + dtb_ref[...])
    beta_ref[...] = _sigmoid(ab)
    rr = lax.broadcasted_iota(jnp.int32, (ts, ts), 0)
    cc = lax.broadcasted_iota(jnp.int32, (ts, ts), 1)
    ltri = ((rr >= cc) & ((rr >> CHUNK_SHIFT) == (cc >> CHUNK_SHIFT))).astype(BF16)
    g1 = graw.astype(BF16)
    r1 = graw - g1.astype(F32)
    g2 = r1.astype(BF16)
    g3 = (r1 - g2.astype(F32)).astype(BF16)
    gc_ref[...] = _dot(ltri, g1) + _dot(ltri, g2) + _dot(ltri, g3)

    ri = lax.broadcasted_iota(jnp.int32, (CHUNK, CHUNK), 0)
    ci = lax.broadcasted_iota(jnp.int32, (CHUNK, CHUNK), 1)
    causal = ri >= ci
    strict = ri > ci
    eye = (ri == ci).astype(F32)
    level_masks = tuple((ri >> sh) == (ci >> sh) for sh in (3, 4, 5, 6))
    nchunk = ts // CHUNK
    probs = [(c, h) for c in range(nchunk) for h in range(GDN_HEADS)]

    def rows_of(c):
        return slice(c * CHUNK, (c + 1) * CHUNK)

    def lanes_of(h):
        return slice(h * HEAD_DIM, (h + 1) * HEAD_DIM)

    def solve_batch(batch, a_batch):
        for (c, h), mi in zip(batch, _unit_lower_inverses(a_batch, eye, level_masks)):
            rows, sl = rows_of(c), lanes_of(h)
            mb = mi.astype(BF16)
            w_ref[rows, sl] = _dot(mb, rk_ref[rows, sl]).astype(BF16)
            u_ref[rows, sl] = _dot(mb, rv_ref[rows, sl])

    egl = []
    for c in range(nchunk):
        gcb = gc_ref[rows_of(c), :]
        gt_ref[rows_of(c), :] = gcb.T
        gl = gcb[CHUNK - 1:CHUNK, :]
        eg_ref[rows_of(c), :] = jnp.exp(gcb)
        ekd_ref[rows_of(c), :] = jnp.exp(gl - gcb)
        egl.append(jnp.exp(gl))

    def intra_chunk(c, h):
        rows, sl = rows_of(c), lanes_of(h)
        qh = qs_ref[rows, sl]
        kh = ks_ref[rows, sl]
        vh = vs_ref[rows, sl]
        gcol = gc_ref[rows, :][:, h:h + 1]
        grow = gt_ref[c * CHUNK + h:c * CHUNK + h + 1, :]
        dec = jnp.exp(jnp.where(causal, gcol - grow, NEG_BIG))
        bcol = beta_ref[rows, :][:, GDN_HEADS + h:GDN_HEADS + h + 1]
        egc = eg_ref[rows, :][:, h:h + 1]
        ekc = ekd_ref[rows, :][:, h:h + 1]
        kb = kh.astype(BF16)
        attn_ref[rows, sl] = (_dot_nt(qh.astype(BF16), kb) * dec).astype(BF16)
        qd_ref[rows, sl] = (qh * egc).astype(BF16)
        kd_ref[rows, sl] = (kh * ekc).astype(BF16)
        rk_ref[rows, sl] = (kh * (bcol * egc)).astype(BF16)
        rv_ref[rows, sl] = (vh * bcol).astype(BF16)
        return jnp.where(strict, bcol * _dot_nt(kb, kb) * dec, 0.0)

    for b0 in range(0, len(probs), GDN_BATCH):
        batch = probs[b0:b0 + GDN_BATCH]
        solve_batch(batch, [intra_chunk(c, h) for c, h in batch])

    for c in range(nchunk):
        rows = rows_of(c)
        st = [st_ref[h] for h in range(GDN_HEADS)]
        sb = [x.astype(BF16) for x in st]
        vn = [u_ref[rows, lanes_of(h)] - _dot(w_ref[rows, lanes_of(h)], sb[h]) for h in range(GDN_HEADS)]
        vnb = [x.astype(BF16) for x in vn]
        for h in range(GDN_HEADS):
            sl = lanes_of(h)
            st_ref[h] = st[h] * egl[c][:, h:h + 1] + _dot_tn(kd_ref[rows, sl], vnb[h])
            os_ref[rows, sl] = _dot(qd_ref[rows, sl], sb[h]) + _dot(attn_ref[rows, sl], vnb[h])

    nw = nw_ref[...]
    for h in range(GDN_HEADS):
        sl = slice(h * HEAD_DIM, (h + 1) * HEAD_DIM)
        o = os_ref[:, sl]
        zz = in_ref[:, COL_Z + h * HEAD_DIM:COL_Z + (h + 1) * HEAD_DIM]
        y = o * lax.rsqrt(jnp.mean(o * o, axis=-1, keepdims=True) + EPS) * nw * _silu(zz)
        o_ref[:, sl] = y.astype(o_ref.dtype)


def _gdn(proj, cw, alog, dtb, nw, *, batch, seq, ts):
    nblk = seq // ts
    small = pl.BlockSpec((1, 128), lambda b, s: (0, 0))
    return pl.pallas_call(
        _gdn_kernel,
        grid=(batch, nblk),
        in_specs=[
            pl.BlockSpec((ts, GDN_IN_WIDTH), lambda b, s: (b * nblk + s, 0)),
            pl.BlockSpec((GDN_CONV, 3 * GDN_WIDTH), lambda b, s: (0, 0)),
            small, small, small,
        ],
        out_specs=pl.BlockSpec((ts, GDN_WIDTH), lambda b, s: (b * nblk + s, 0)),
        out_shape=jax.ShapeDtypeStruct((batch * seq, GDN_WIDTH), BF16),
        scratch_shapes=(
            [pltpu.VMEM((ts + HALO, 3 * GDN_WIDTH), F32)]
            + [pltpu.VMEM((ts, GDN_WIDTH), F32)] * 3
            + [pltpu.VMEM((ts, 128), F32)] * 5
            + [pltpu.VMEM((ts, GDN_WIDTH), BF16)] * 6
            + [pltpu.VMEM((ts, GDN_WIDTH), F32)] * 2
            + [pltpu.VMEM((GDN_HEADS, HEAD_DIM, HEAD_DIM), F32)]
        ),
        compiler_params=_cparams(("arbitrary", "arbitrary")),
        name="gdn",
    )(proj, cw, alog, dtb, nw)


def _outproj_kernel(x_ref, yp_ref, yg_ref, yl_ref, w_ref, o_ref):
    mixed = jnp.concatenate([yp_ref[...], yg_ref[...], yl_ref[...]], axis=1)
    o_ref[...] = x_ref[...] + _dot(mixed, w_ref[...])


def _outproj(x, yp, yg, yl, w, layer, *, tm):
    t, d = x.shape

    def rows(width):
        return pl.BlockSpec((tm, width), lambda i: (i, 0))

    return pl.pallas_call(
        _outproj_kernel,
        grid=(t // tm,),
        in_specs=[rows(d), rows(POOL_WIDTH), rows(GDN_WIDTH), rows(LRU_WIDTH),
                  pl.BlockSpec((None, d, d), lambda i: (layer, 0, 0))],
        out_specs=rows(d),
        out_shape=jax.ShapeDtypeStruct((t, d), F32),
        compiler_params=_cparams(("arbitrary",)),
        name="outproj",
    )(x, yp, yg, yl, w)


def _ffn_up_kernel(x_ref, nw_ref, wg_ref, wv_ref, cw_ref, o_ref, h_ref, g_ref, carry_ref,
                   *, tiles_per_seq, rb):
    i = pl.program_id(0)
    f = pl.program_id(1)
    tm = x_ref.shape[0]

    @pl.when(f == 0)
    def _():
        h_ref[...] = _rmsnorm(x_ref[...], nw_ref[...]).astype(BF16)

    halo = carry_ref[f]
    seq_start = (i % tiles_per_seq) == 0
    g_ref[0:HALO, :] = jnp.where(seq_start, jnp.zeros_like(halo), halo)
    cw = cw_ref[...]
    wg = wg_ref[...].astype(BF16)
    wv = wv_ref[...].astype(BF16)
    for r in range(tm // rb):
        rows = slice(r * rb, (r + 1) * rb)
        hb = h_ref[rows, :]
        g_ref[HALO + r * rb:HALO + (r + 1) * rb, :] = _dot(hb, wg)
        val = _dot(hb, wv)
        conv = _conv_from_ext(g_ref.at[r * rb:r * rb + HALO + rb, :], 0, g_ref.shape[1], cw, FFN_CONV, rb)
        o_ref[rows, :] = (_gelu(conv) * val).astype(o_ref.dtype)
    carry_ref[f] = g_ref[tm:tm + HALO, :]


def _ffn_up(x, nw, w_up, cw, layer, *, seq, tm, tf, rb):
    t, d = x.shape
    nf = D_FF // tf
    kern = functools.partial(_ffn_up_kernel, tiles_per_seq=seq // tm, rb=rb)
    return pl.pallas_call(
        kern,
        grid=(t // tm, nf),
        in_specs=[
            pl.BlockSpec((tm, d), lambda i, f: (i, 0)),
            pl.BlockSpec((1, d), lambda i, f: (0, 0)),
            pl.BlockSpec((None, d, tf), lambda i, f: (layer, 0, f)),
            pl.BlockSpec((None, d, tf), lambda i, f: (layer, 0, nf + f)),
            pl.BlockSpec((None, FFN_CONV, tf), lambda i, f: (layer, 0, f)),
        ],
        out_specs=pl.BlockSpec((tm, tf), lambda i, f: (i, f)),
        out_shape=jax.ShapeDtypeStruct((t, D_FF), BF16),
        scratch_shapes=[
            pltpu.VMEM((tm, d), BF16),
            pltpu.VMEM((tm + HALO, tf), F32),
            pltpu.VMEM((nf, HALO, tf), F32),
        ],
        compiler_params=_cparams(("arbitrary", "arbitrary")),
        name="ffn_up",
    )(x, nw, w_up, w_up, cw)


def _ffn_down_kernel(a_ref, w_ref, x_ref, fnw_ref, o_ref, *, nj, final_norm):
    j = pl.program_id(1)
    tn = w_ref.shape[1]
    half = tn // 2
    a = a_ref[...]
    halves = [(c, x_ref[:, c:c + half] + _dot(a, w_ref[:, c:c + half])) for c in (0, half)]
    if not final_norm:
        for c, y in halves:
            o_ref[:, c:c + half] = y
        return
    for jj in range(nj):
        @pl.when(j == jj)
        def _():
            for c, y in halves:
                o_ref[:, jj * tn + c:jj * tn + c + half] = y

    @pl.when(j == nj - 1)
    def _():
        o_ref[...] = _rmsnorm(o_ref[...], fnw_ref[...])


def _ffn_down(act, w_down, x, fnw, layer, *, tm, tn, final_norm):
    t, d = x.shape
    nj = d // tn
    kern = functools.partial(_ffn_down_kernel, nj=nj, final_norm=final_norm)
    return pl.pallas_call(
        kern,
        grid=(t // tm, nj),
        in_specs=[
            pl.BlockSpec((tm, D_FF), lambda i, j: (i, 0)),
            pl.BlockSpec((None, D_FF, tn), lambda i, j: (layer, 0, j)),
            pl.BlockSpec((tm, tn), lambda i, j: (i, j)),
            pl.BlockSpec((1, d), lambda i, j: (0, 0)),
        ],
        out_specs=(pl.BlockSpec((tm, d), lambda i, j: (i, 0)) if final_norm
                   else pl.BlockSpec((tm, tn), lambda i, j: (i, j))),
        out_shape=jax.ShapeDtypeStruct((t, d), F32),
        compiler_params=_cparams(("arbitrary", "arbitrary")),
        name="ffn_down",
    )(act, w_down, x, fnw)


def _pad_cols(w, width):
    return jnp.pad(w, ((0, 0), (0, width - w.shape[1])))


def _prep_w_in(w, layer):
    wt = w[layer].T
    return jnp.pad(wt, ((0, PROJ_COLS - wt.shape[0]), (0, 0))).astype(BF16)


TILES = dict(
    inproj=dict(tm=1024, tn=1792, norm_rows=256),
    pool=dict(tp=1024),
    gdn=dict(ts=2 * CHUNK),
    lru=dict(tl=512),
    outproj=dict(tm=512),
    ffn_up=dict(tm=1024, tf=768, rb=512),
    ffn_down=dict(tm=1024, tn=512),
    ffn_down_final=dict(tm=512, tn=1024),
)


def _row(v, width=None):
    v = v.reshape(1, -1).astype(F32)
    return v if width is None else _pad_cols(v, width)


def kernel(x, norm1_w, w_in, pool_w, pool_b, pool_scale, gdn_conv_w, gdn_a_log, gdn_dt_bias, gdn_norm_w,
           lru_conv_w, lru_conv_b, lru_wa, lru_ba, lru_wx, lru_bx, lru_lambda, w_out, norm2_w, ffn_up,
           ffn_conv_w, ffn_down, final_norm_w):
    batch, seq, d = x.shape
    xf = x.reshape(batch * seq, d)
    w_out_b = w_out.astype(BF16)
    w_down_b = ffn_down.astype(BF16)
    fnw = _row(final_norm_w)
    for l in range(DEPTH):
        proj = _inproj(xf, _row(norm1_w[l]), _prep_w_in(w_in, l), **TILES["inproj"])
        y_pool = _pool(proj, pool_w[l].astype(BF16), _row(pool_b[l]), _row(pool_scale[l]),
                       batch=batch, seq=seq, **TILES["pool"])
        y_gdn = _gdn(proj, gdn_conv_w[l], _row(gdn_a_log[l], 128), _row(gdn_dt_bias[l], 128),
                     _row(gdn_norm_w[l]), batch=batch, seq=seq, **TILES["gdn"])
        y_lru = _lru(proj, lru_conv_w[l], _row(lru_conv_b[l]), lru_wa[l].astype(BF16), _row(lru_ba[l]),
                     lru_wx[l].astype(BF16), _row(lru_bx[l]), _row(lru_lambda[l]),
                     batch=batch, seq=seq, **TILES["lru"])
        x1 = _outproj(xf, y_pool, y_gdn, y_lru, w_out_b, l, **TILES["outproj"])
        act = _ffn_up(x1, _row(norm2_w[l]), ffn_up, ffn_conv_w, l, seq=seq, **TILES["ffn_up"])
        last = l == DEPTH - 1
        xf = _ffn_down(act, w_down_b, x1, fnw, l, final_norm=last,
                       **TILES["ffn_down_final" if last else "ffn_down"])
    return xf.reshape(batch, seq, d)
```

```python
import functools

import jax
import jax.numpy as jnp
from jax import lax
from jax.experimental import pallas as pl
from jax.experimental.pallas import tpu as pltpu

F32 = jnp.float32
BF16 = jnp.bfloat16

D_MODEL = 2048
DEPTH = 2
POOL_WINDOWS = (2, 4, 8, 16)
POOL_GROUP_DIM = 128
POOL_WIDTH = 512
GDN_HEADS = 6
HEAD_DIM = 128
GDN_WIDTH = 768
GDN_CONV = 4
LRU_BLOCKS = 6
LRU_WIDTH = 768
LRU_CONV = 4
LRU_C = 8.0
D_FF = 3 * D_MODEL
FFN_CONV = 3
EPS = 1e-6

LANES = 128
COL_POOL = 0
COL_Q, COL_Z = 512, 2816
COL_AB = 3584
COL_XR = 3596
PROJ_COLS = 5376
GDN_IN_WIDTH = COL_AB + LANES
LRU_IN_WIDTH = PROJ_COLS - COL_AB
assert COL_AB % LRU_IN_WIDTH == 0 and LRU_IN_WIDTH % LANES == 0 and GDN_IN_WIDTH % LANES == 0

SUBLANES = 8
HALO = SUBLANES
POOL_HALO = 2 * SUBLANES
assert POOL_HALO >= max(POOL_WINDOWS) - 1
CHUNK_SHIFT = 7
CHUNK = 1 << CHUNK_SHIFT
NEG_BIG = -1e30
GDN_BATCH = 12

V7X_VMEM_BYTES = 64 * 1024 * 1024
V7X_VMEM_LIMIT = V7X_VMEM_BYTES - 4 * 1024 * 1024


def _cparams(sem):
    return pltpu.CompilerParams(dimension_semantics=sem, vmem_limit_bytes=V7X_VMEM_LIMIT)


def _dot(a, b):
    return jnp.dot(a, b, preferred_element_type=F32)


def _dot_nt(a, b):
    return lax.dot_general(a, b, (((1,), (1,)), ((), ())), preferred_element_type=F32)


def _dot_tn(a, b):
    return lax.dot_general(a, b, (((0,), (0,)), ((), ())), preferred_element_type=F32)


def _sigmoid(x):
    return 0.5 + 0.5 * jnp.tanh(0.5 * x)


def _silu(x):
    h = 0.5 * x
    return h + h * jnp.tanh(h)


def _softplus(x):
    return jnp.maximum(x, 0.0) + jnp.log1p(jnp.exp(-jnp.abs(x)))


def _gelu(x):
    return jax.nn.gelu(x, approximate=True)


def _rmsnorm(x, w):
    return x * lax.rsqrt(jnp.mean(x * x, axis=-1, keepdims=True) + EPS) * w


def _conv_from_ext(ext_ref, col0, width, w, taps, rows):
    x = ext_ref[0:HALO + rows, col0:col0 + width]

    def tap(lag):
        return w[taps - 1 - lag:taps - lag, :]

    if taps == 4:
        x1 = pltpu.roll(x, 1, 0)
        y01 = x * tap(0) + x1 * tap(1)
        y23 = x * tap(2) + x1 * tap(3)
        return y01[HALO:] + pltpu.roll(y23, 2, 0)[HALO:]
    acc = x[HALO:] * tap(0)
    for lag in range(1, taps):
        shifted = pltpu.roll(x, lag, 0)[HALO:]
        acc = acc + shifted * tap(lag)
    return acc


def _inproj_kernel(x_ref, nw_ref, w_ref, o_ref, h_ref, *, norm_rows):
    j = pl.program_id(1)
    tm = x_ref.shape[0]

    @pl.when(j == 0)
    def _():
        w = w_ref[...]
        for r0 in range(0, tm, norm_rows):
            rows = slice(r0, r0 + norm_rows)
            h = _rmsnorm(x_ref[rows, :], nw_ref[...]).astype(BF16)
            h_ref[rows, :] = h
            o_ref[rows, :] = _dot_nt(h, w)

    @pl.when(j > 0)
    def _():
        o_ref[...] = _dot_nt(h_ref[...], w_ref[...])


def _inproj(x, nw, w, *, tm, tn, norm_rows):
    t, d = x.shape
    n = w.shape[0]
    return pl.pallas_call(
        functools.partial(_inproj_kernel, norm_rows=norm_rows),
        grid=(t // tm, n // tn),
        in_specs=[
            pl.BlockSpec((tm, d), lambda i, j: (i, 0)),
            pl.BlockSpec((1, d), lambda i, j: (0, 0)),
            pl.BlockSpec((tn, d), lambda i, j: (j, 0)),
        ],
        out_specs=pl.BlockSpec((tm, tn), lambda i, j: (i, j)),
        out_shape=jax.ShapeDtypeStruct((t, n), F32),
        scratch_shapes=[pltpu.VMEM((tm, d), BF16)],
        compiler_params=_cparams(("arbitrary", "arbitrary")),
        name="inproj",
    )(x, nw, w)


def _pool_kernel(u_ref, w_ref, b_ref, sc_ref, o_ref, ext_ref):
    s = pl.program_id(1)
    tp = u_ref.shape[0]

    @pl.when(s == 0)
    def _():
        ext_ref[0:POOL_HALO, :] = jnp.zeros((POOL_HALO, POOL_WIDTH), F32)

    @pl.when(s > 0)
    def _():
        ext_ref[0:POOL_HALO, :] = ext_ref[tp:tp + POOL_HALO, :]

    ext_ref[POOL_HALO:POOL_HALO + tp, :] = u_ref[...]
    pos = s * tp + lax.broadcasted_iota(jnp.int32, (tp, 1), 0)

    for g, win in enumerate(POOL_WINDOWS):
        sl = slice(g * POOL_GROUP_DIM, (g + 1) * POOL_GROUP_DIM)
        x = ext_ref[:, sl]
        acc = x
        span = 1
        while span < win:
            acc = acc + pltpu.roll(acc, span, 0)
            span *= 2
        wsum = acc[POOL_HALO:]
        u = x[POOL_HALO:]
        cnt = jnp.minimum(pos + 1, win).astype(F32)
        d = wsum / cnt - u
        y = _dot(d.astype(BF16), w_ref[g]) + b_ref[:, sl]
        o_ref[:, sl] = (y * sc_ref[:, sl]).astype(o_ref.dtype)


def _pool(proj, pw, pb, psc, *, batch, seq, tp):
    nblk = seq // tp
    return pl.pallas_call(
        _pool_kernel,
        grid=(batch, nblk),
        in_specs=[
            pl.BlockSpec((tp, POOL_WIDTH), lambda b, s: (b * nblk + s, COL_POOL // POOL_WIDTH)),
            pl.BlockSpec((4, 128, 128), lambda b, s: (0, 0, 0)),
            pl.BlockSpec((1, POOL_WIDTH), lambda b, s: (0, 0)),
            pl.BlockSpec((1, POOL_WIDTH), lambda b, s: (0, 0)),
        ],
        out_specs=pl.BlockSpec((tp, POOL_WIDTH), lambda b, s: (b * nblk + s, 0)),
        out_shape=jax.ShapeDtypeStruct((batch * seq, POOL_WIDTH), BF16),
        scratch_shapes=[pltpu.VMEM((tp + POOL_HALO, POOL_WIDTH), F32)],
        compiler_params=_cparams(("arbitrary", "arbitrary")),
        name="pool",
    )(proj, pw, pb, psc)


def _lru_kernel(in_ref, cw_ref, cb_ref, wa_ref, ba_ref, wx_ref, bx_ref, lam_ref,
                o_ref, ext_ref, gr_ref, hc_ref):
    s = pl.program_id(1)
    tl = in_ref.shape[0]

    @pl.when(s == 0)
    def _():
        ext_ref[0:HALO, :] = jnp.zeros((HALO, LRU_WIDTH), F32)
        hc_ref[...] = jnp.zeros_like(hc_ref)

    @pl.when(s > 0)
    def _():
        ext_ref[0:HALO, :] = ext_ref[tl:tl + HALO, :]

    aligned = pltpu.roll(in_ref[...], LRU_IN_WIDTH - (COL_XR - COL_AB), 1)
    ext_ref[HALO:HALO + tl, :] = aligned[:, 0:LRU_WIDTH]
    gr_ref[...] = aligned[:, LRU_WIDTH:2 * LRU_WIDTH]

    row = lax.broadcasted_iota(jnp.int32, (tl, 128), 0)
    first = (row == 0) & (s == 0)
    groups = tl // SUBLANES
    sub = lax.broadcasted_iota(jnp.int32, (groups, SUBLANES, 128), 1)
    nsp = -LRU_C * _softplus(-lam_ref[...])
    cw = cw_ref[...]

    for blk in range(LRU_BLOCKS):
        sl = slice(blk * 128, (blk + 1) * 128)
        xc = _conv_from_ext(ext_ref, blk * 128, 128, cw[:, sl], LRU_CONV, tl) + cb_ref[:, sl]
        xb = xc.astype(BF16)
        r = _sigmoid(_dot(xb, wa_ref[blk]) + ba_ref[:, sl])
        i = _sigmoid(_dot(xb, wx_ref[blk]) + bx_ref[:, sl])
        log_a = r * nsp[:, sl]
        a = jnp.exp(log_a)
        mult = jnp.sqrt(-jnp.tanh(log_a) * (a * a + 1.0))
        mult = jnp.where(first, 1.0, mult)
        bv = mult * i * xc
        a3 = a.reshape(groups, SUBLANES, 128)
        b3 = bv.reshape(groups, SUBLANES, 128)
        sh = 1
        while sh < SUBLANES:
            keep = sub >= sh
            a_sh = jnp.where(keep, pltpu.roll(a3, sh, 1), 1.0)
            b_sh = jnp.where(keep, pltpu.roll(b3, sh, 1), 0.0)
            b3 = a3 * b_sh + b3
            a3 = a3 * a_sh
            sh *= 2
        hprev = hc_ref[:, sl]
        hs = []
        for g in range(groups):
            hg = a3[g] * hprev + b3[g]
            hs.append(hg)
            hprev = hg[SUBLANES - 1:SUBLANES, :]
        hc_ref[:, sl] = hprev
        h = jnp.concatenate(hs, axis=0)
        o_ref[:, sl] = (h * _gelu(gr_ref[:, sl])).astype(o_ref.dtype)


def _lru(proj, cw, cb, wa, ba, wx, bx, lam, *, batch, seq, tl):
    nblk = seq // tl
    vec = pl.BlockSpec((1, LRU_WIDTH), lambda b, s: (0, 0))
    mat = pl.BlockSpec((LRU_BLOCKS, 128, 128), lambda b, s: (0, 0, 0))
    return pl.pallas_call(
        _lru_kernel,
        grid=(batch, nblk),
        in_specs=[
            pl.BlockSpec((tl, LRU_IN_WIDTH), lambda b, s: (b * nblk + s, COL_AB // LRU_IN_WIDTH)),
            pl.BlockSpec((LRU_CONV, LRU_WIDTH), lambda b, s: (0, 0)),
            vec, mat, vec, mat, vec, vec,
        ],
        out_specs=pl.BlockSpec((tl, LRU_WIDTH), lambda b, s: (b * nblk + s, 0)),
        out_shape=jax.ShapeDtypeStruct((batch * seq, LRU_WIDTH), BF16),
        scratch_shapes=[pltpu.VMEM((tl + HALO, LRU_WIDTH), F32), pltpu.VMEM((tl, LRU_WIDTH), F32),
                        pltpu.VMEM((1, LRU_WIDTH), F32)],
        compiler_params=_cparams(("arbitrary", "arbitrary")),
        name="lru",
    )(proj, cw, cb, wa, ba, wx, bx, lam)


def _unit_lower_inverses(a_list, eye, level_masks):
    m8 = level_masks[0]
    zero = jnp.zeros((), BF16)
    ab = [a.astype(BF16) for a in a_list]
    n1 = [jnp.where(m8, -a, zero) for a in ab]
    p = [eye + n.astype(F32) for n in n1]
    n2 = [_dot(n, n).astype(BF16) for n in n1]
    p = [pi + _dot(pi.astype(BF16), n) for pi, n in zip(p, n2)]
    n4 = [_dot(n, n).astype(BF16) for n in n2]
    p = [pi + _dot(pi.astype(BF16), n) for pi, n in zip(p, n4)]
    for lvl in range(len(level_masks)):
        inner = level_masks[lvl]
        sel = ~inner if lvl + 1 == len(level_masks) else (level_masks[lvl + 1] & ~inner)
        pb = [pi.astype(BF16) for pi in p]
        t = [_dot(pbi, jnp.where(sel, a, zero)).astype(BF16) for pbi, a in zip(pb, ab)]
        p = [pi - _dot(ti, pbi) for pi, ti, pbi in zip(p, t, pb)]
    return p


def _gdn_kernel(in_ref, cw_ref, alog_ref, dtb_ref, nw_ref,
                o_ref, ext_ref, qs_ref, ks_ref, vs_ref, gc_ref, beta_ref, gt_ref, eg_ref, ekd_ref,
                attn_ref, qd_ref, kd_ref, rk_ref, rv_ref, w_ref, u_ref, os_ref, st_ref):
    s = pl.program_id(1)
    ts = in_ref.shape[0]

    @pl.when(s == 0)
    def _():
        ext_ref[0:HALO, :] = jnp.zeros((HALO, 3 * GDN_WIDTH), F32)
        st_ref[...] = jnp.zeros_like(st_ref)

    @pl.when(s > 0)
    def _():
        ext_ref[0:HALO, :] = ext_ref[ts:ts + HALO, :]

    ext_ref[HALO:HALO + ts, :] = in_ref[:, COL_Q:COL_Z]

    cw = cw_ref[...]
    for h in range(GDN_HEADS):
        for ti, dst in enumerate((qs_ref, ks_ref, vs_ref)):
            c0 = ti * GDN_WIDTH + h * HEAD_DIM
            t = _conv_from_ext(ext_ref, c0, HEAD_DIM, cw[:, c0:c0 + HEAD_DIM], GDN_CONV, ts)
            t = _silu(t)
            if ti < 2:
                inv = lax.rsqrt(jnp.sum(t * t, axis=-1, keepdims=True) + EPS)
                if ti == 0:
                    inv = inv * (HEAD_DIM ** -0.5)
                t = t * inv
            dst[:, h * HEAD_DIM:(h + 1) * HEAD_DIM] = t

    ab = in_ref[:, COL_AB:COL_AB + LANES]
    graw = -jnp.exp(alog_ref[...]) * _softplus(ab + dtb_ref[...])
    beta_ref[...] = _sigmoid(ab)
    rr = lax.broadcasted_iota(jnp.int32, (ts, ts), 0)
    cc = lax.broadcasted_iota(jnp.int32, (ts, ts), 1)
    ltri = ((rr >= cc) & ((rr >> CHUNK_SHIFT) == (cc >> CHUNK_SHIFT))).astype(BF16)
    g1 = graw.astype(BF16)
    r1 = graw - g1.astype(F32)
    g2 = r1.astype(BF16)
    g3 = (r1 - g2.astype(F32)).astype(BF16)
    gc_ref[...] = _dot(ltri, g1) + _dot(ltri, g2) + _dot(ltri, g3)

    ri = lax.broadcasted_iota(jnp.int32, (CHUNK, CHUNK), 0)
    ci = lax.broadcasted_iota(jnp.int32, (CHUNK, CHUNK), 1)
    causal = ri >= ci
    strict = ri > ci
    eye = (ri == ci).astype(F32)
    level_masks = tuple((ri >> sh) == (ci >> sh) for sh in (3, 4, 5, 6))
    nchunk = ts // CHUNK
    probs = [(c, h) for c in range(nchunk) for h in range(GDN_HEADS)]

    def rows_of(c):
        return slice(c * CHUNK, (c + 1) * CHUNK)

    def lanes_of(h):
        return slice(h * HEAD_DIM, (h + 1) * HEAD_DIM)

    def solve_batch(batch, a_batch):
        for (c, h), mi in zip(batch, _unit_lower_inverses(a_batch, eye, level_masks)):
            rows, sl = rows_of(c), lanes_of(h)
            mb = mi.astype(BF16)
            w_ref[rows, sl] = _dot(mb, rk_ref[rows, sl]).astype(BF16)
            u_ref[rows, sl] = _dot(mb, rv_ref[rows, sl])

    egl = []
    for c in range(nchunk):
        gcb = gc_ref[rows_of(c), :]
        gt_ref[rows_of(c), :] = gcb.T
        gl = gcb[CHUNK - 1:CHUNK, :]
        eg_ref[rows_of(c), :] = jnp.exp(gcb)
        ekd_ref[rows_of(c), :] = jnp.exp(gl - gcb)
        egl.append(jnp.exp(gl))

    def intra_chunk(c, h):
        rows, sl = rows_of(c), lanes_of(h)
        qh = qs_ref[rows, sl]
        kh = ks_ref[rows, sl]
        vh = vs_ref[rows, sl]
        gcol = gc_ref[rows, :][:, h:h + 1]
        grow = gt_ref[c * CHUNK + h:c * CHUNK + h + 1, :]
        dec = jnp.exp(jnp.where(causal, gcol - grow, NEG_BIG))
        bcol = beta_ref[rows, :][:, GDN_HEADS + h:GDN_HEADS + h + 1]
        egc = eg_ref[rows, :][:, h:h + 1]
        ekc = ekd_ref[rows, :][:, h:h + 1]
        kb = kh.astype(BF16)
        attn_ref[rows, sl] = (_dot_nt(qh.astype(BF16), kb) * dec).astype(BF16)
        qd_ref[rows, sl] = (qh * egc).astype(BF16)
        kd_ref[rows, sl] = (kh * ekc).astype(BF16)
        rk_ref[rows, sl] = (kh * (bcol * egc)).astype(BF16)
        rv_ref[rows, sl] = (vh * bcol).astype(BF16)
        return jnp.where(strict, bcol * _dot_nt(kb, kb) * dec, 0.0)

    for b0 in range(0, len(probs), GDN_BATCH):
        batch = probs[b0:b0 + GDN_BATCH]
        solve_batch(batch, [intra_chunk(c, h) for c, h in batch])

    for c in range(nchunk):
        rows = rows_of(c)
        st = [st_ref[h] for h in range(GDN_HEADS)]
        sb = [x.astype(BF16) for x in st]
        vn = [u_ref[rows, lanes_of(h)] - _dot(w_ref[rows, lanes_of(h)], sb[h]) for h in range(GDN_HEADS)]
        vnb = [x.astype(BF16) for x in vn]
        for h in range(GDN_HEADS):
            sl = lanes_of(h)
            st_ref[h] = st[h] * egl[c][:, h:h + 1] + _dot_tn(kd_ref[rows, sl], vnb[h])
            os_ref[rows, sl] = _dot(qd_ref[rows, sl], sb[h]) + _dot(attn_ref[rows, sl], vnb[h])

    nw = nw_ref[...]
    for h in range(GDN_HEADS):
        sl = slice(h * HEAD_DIM, (h + 1) * HEAD_DIM)
        o = os_ref[:, sl]
        zz = in_ref[:, COL_Z + h * HEAD_DIM:COL_Z + (h + 1) * HEAD_DIM]
        y = o * lax.rsqrt(jnp.mean(o * o, axis=-1, keepdims=True) + EPS) * nw * _silu(zz)
        o_ref[:, sl] = y.astype(o_ref.dtype)


def _gdn(proj, cw, alog, dtb, nw, *, batch, seq, ts):
    nblk = seq // ts
    small = pl.BlockSpec((1, 128), lambda b, s: (0, 0))
    return pl.pallas_call(
        _gdn_kernel,
        grid=(batch, nblk),
        in_specs=[
            pl.BlockSpec((ts, GDN_IN_WIDTH), lambda b, s: (b * nblk + s, 0)),
            pl.BlockSpec((GDN_CONV, 3 * GDN_WIDTH), lambda b, s: (0, 0)),
            small, small, small,
        ],
        out_specs=pl.BlockSpec((ts, GDN_WIDTH), lambda b, s: (b * nblk + s, 0)),
        out_shape=jax.ShapeDtypeStruct((batch * seq, GDN_WIDTH), BF16),
        scratch_shapes=(
            [pltpu.VMEM((ts + HALO, 3 * GDN_WIDTH), F32)]
            + [pltpu.VMEM((ts, GDN_WIDTH), F32)] * 3
            + [pltpu.VMEM((ts, 128), F32)] * 5
            + [pltpu.VMEM((ts, GDN_WIDTH), BF16)] * 6
            + [pltpu.VMEM((ts, GDN_WIDTH), F32)] * 2
            + [pltpu.VMEM((GDN_HEADS, HEAD_DIM, HEAD_DIM), F32)]
        ),
        compiler_params=_cparams(("arbitrary", "arbitrary")),
        name="gdn",
    )(proj, cw, alog, dtb, nw)


def _outproj_kernel(x_ref, yp_ref, yg_ref, yl_ref, w_ref, o_ref):
    mixed = jnp.concatenate([yp_ref[...], yg_ref[...], yl_ref[...]], axis=1)
    o_ref[...] = x_ref[...] + _dot(mixed, w_ref[...])


def _outproj(x, yp, yg, yl, w, layer, *, tm):
    t, d = x.shape

    def rows(width):
        return pl.BlockSpec((tm, width), lambda i: (i, 0))

    return pl.pallas_call(
        _outproj_kernel,
        grid=(t // tm,),
        in_specs=[rows(d), rows(POOL_WIDTH), rows(GDN_WIDTH), rows(LRU_WIDTH),
                  pl.BlockSpec((None, d, d), lambda i: (layer, 0, 0))],
        out_specs=rows(d),
        out_shape=jax.ShapeDtypeStruct((t, d), F32),
        compiler_params=_cparams(("arbitrary",)),
        name="outproj",
    )(x, yp, yg, yl, w)


def _ffn_up_kernel(x_ref, nw_ref, wg_ref, wv_ref, cw_ref, o_ref, h_ref, g_ref, carry_ref,
                   *, tiles_per_seq, rb):
    i = pl.program_id(0)
    f = pl.program_id(1)
    tm = x_ref.shape[0]

    @pl.when(f == 0)
    def _():
        h_ref[...] = _rmsnorm(x_ref[...], nw_ref[...]).astype(BF16)

    halo = carry_ref[f]
    seq_start = (i % tiles_per_seq) == 0
    g_ref[0:HALO, :] = jnp.where(seq_start, jnp.zeros_like(halo), halo)
    cw = cw_ref[...]
    wg = wg_ref[...].astype(BF16)
    wv = wv_ref[...].astype(BF16)
    for r in range(tm // rb):
        rows = slice(r * rb, (r + 1) * rb)
        hb = h_ref[rows, :]
        g_ref[HALO + r * rb:HALO + (r + 1) * rb, :] = _dot(hb, wg)
        val = _dot(hb, wv)
        conv = _conv_from_ext(g_ref.at[r * rb:r * rb + HALO + rb, :], 0, g_ref.shape[1], cw, FFN_CONV, rb)
        o_ref[rows, :] = (_gelu(conv) * val).astype(o_ref.dtype)
    carry_ref[f] = g_ref[tm:tm + HALO, :]


def _ffn_up(x, nw, w_up, cw, layer, *, seq, tm, tf, rb):
    t, d = x.shape
    nf = D_FF // tf
    kern = functools.partial(_ffn_up_kernel, tiles_per_seq=seq // tm, rb=rb)
    return pl.pallas_call(
        kern,
        grid=(t // tm, nf),
        in_specs=[
            pl.BlockSpec((tm, d), lambda i, f: (i, 0)),
            pl.BlockSpec((1, d), lambda i, f: (0, 0)),
            pl.BlockSpec((None, d, tf), lambda i, f: (layer, 0, f)),
            pl.BlockSpec((None, d, tf), lambda i, f: (layer, 0, nf + f)),
            pl.BlockSpec((None, FFN_CONV, tf), lambda i, f: (layer, 0, f)),
        ],
        out_specs=pl.BlockSpec((tm, tf), lambda i, f: (i, f)),
        out_shape=jax.ShapeDtypeStruct((t, D_FF), BF16),
        scratch_shapes=[
            pltpu.VMEM((tm, d), BF16),
            pltpu.VMEM((tm + HALO, tf), F32),
            pltpu.VMEM((nf, HALO, tf), F32),
        ],
        compiler_params=_cparams(("arbitrary", "arbitrary")),
        name="ffn_up",
    )(x, nw, w_up, w_up, cw)


def _ffn_down_kernel(a_ref, w_ref, x_ref, fnw_ref, o_ref, *, nj, final_norm):
    j = pl.program_id(1)
    tn = w_ref.shape[1]
    half = tn // 2
    a = a_ref[...]
    halves = [(c, x_ref[:, c:c + half] + _dot(a, w_ref[:, c:c + half])) for c in (0, half)]
    if not final_norm:
        for c, y in halves:
            o_ref[:, c:c + half] = y
        return
    for jj in range(nj):
        @pl.when(j == jj)
        def _():
            for c, y in halves:
                o_ref[:, jj * tn + c:jj * tn + c + half] = y

    @pl.when(j == nj - 1)
    def _():
        o_ref[...] = _rmsnorm(o_ref[...], fnw_ref[...])


def _ffn_down(act, w_down, x, fnw, layer, *, tm, tn, final_norm):
    t, d = x.shape
    nj = d // tn
    kern = functools.partial(_ffn_down_kernel, nj=nj, final_norm=final_norm)
    return pl.pallas_call(
        kern,
        grid=(t // tm, nj),
        in_specs=[
            pl.BlockSpec((tm, D_FF), lambda i, j: (i, 0)),
            pl.BlockSpec((None, D_FF, tn), lambda i, j: (layer, 0, j)),
            pl.BlockSpec((tm, tn), lambda i, j: (i, j)),
            pl.BlockSpec((1, d), lambda i, j: (0, 0)),
        ],
        out_specs=(pl.BlockSpec((tm, d), lambda i, j: (i, 0)) if final_norm
                   else pl.BlockSpec((tm, tn), lambda i, j: (i, j))),
        out_shape=jax.ShapeDtypeStruct((t, d), F32),
        compiler_params=_cparams(("arbitrary", "arbitrary")),
        name="ffn_down",
    )(act, w_down, x, fnw)


def _pad_cols(w, width):
    return jnp.pad(w, ((0, 0), (0, width - w.shape[1])))


def _prep_w_in(w, layer):
    wt = w[layer].T
    return jnp.pad(wt, ((0, PROJ_COLS - wt.shape[0]), (0, 0))).astype(BF16)


TILES = dict(
    inproj=dict(tm=1024, tn=1792, norm_rows=256),
    pool=dict(tp=1024),
    gdn=dict(ts=2 * CHUNK),
    lru=dict(tl=512),
    outproj=dict(tm=1024),
    ffn_up=dict(tm=1024, tf=768, rb=512),
    ffn_down=dict(tm=1024, tn=512),
    ffn_down_final=dict(tm=512, tn=1024),
)


def _row(v, width=None):
    v = v.reshape(1, -1).astype(F32)
    return v if width is None else _pad_cols(v, width)


def kernel(x, norm1_w, w_in, pool_w, pool_b, pool_scale, gdn_conv_w, gdn_a_log, gdn_dt_bias, gdn_norm_w,
           lru_conv_w, lru_conv_b, lru_wa, lru_ba, lru_wx, lru_bx, lru_lambda, w_out, norm2_w, ffn_up,
           ffn_conv_w, ffn_down, final_norm_w):
    batch, seq, d = x.shape
    xf = x.reshape(batch * seq, d)
    w_out_b = w_out.astype(BF16)
    w_down_b = ffn_down.astype(BF16)
    fnw = _row(final_norm_w)
    for l in range(DEPTH):
        proj = _inproj(xf, _row(norm1_w[l]), _prep_w_in(w_in, l), **TILES["inproj"])
        y_pool = _pool(proj, pool_w[l].astype(BF16), _row(pool_b[l]), _row(pool_scale[l]),
                       batch=batch, seq=seq, **TILES["pool"])
        y_gdn = _gdn(proj, gdn_conv_w[l], _row(gdn_a_log[l], 128), _row(gdn_dt_bias[l], 128),
                     _row(gdn_norm_w[l]), batch=batch, seq=seq, **TILES["gdn"])
        y_lru = _lru(proj, lru_conv_w[l], _row(lru_conv_b[l]), lru_wa[l].astype(BF16), _row(lru_ba[l]),
                     lru_wx[l].astype(BF16), _row(lru_bx[l]), _row(lru_lambda[l]),
                     batch=batch, seq=seq, **TILES["lru"])
        x1 = _outproj(xf, y_pool, y_gdn, y_lru, w_out_b, l, **TILES["outproj"])
        act = _ffn_up(x1, _row(norm2_w[l]), ffn_up, ffn_conv_w, l, seq=seq, **TILES["ffn_up"])
        last = l == DEPTH - 1
        xf = _ffn_down(act, w_down_b, x1, fnw, l, final_norm=last,
                       **TILES["ffn_down_final" if last else "ffn_down"])
    return xf.reshape(batch, seq, d)
```
